```python
import jax, jax.numpy as jnp
from jax import lax
import numpy as np

D_MODEL = 2048
BATCH = 16
SEQ = 2048
DEPTH = 1
DEC_BATCH = 32
DEC_SEQ = 8
PAST_LEN = 16384
PAGE_SIZE = 128

HEAD_DIM = 128
H_DELTA = 8
H_ATTN = 8
W_DELTA = H_DELTA * HEAD_DIM
W_ATTN = H_ATTN * HEAD_DIM
MIX_WIDTH = W_DELTA + W_ATTN
CONV_W = 4
CHUNK = 64
DIL_PAIRS = ((128, 1), (512, 4), (2048, 16))
WIN_MAX = 2048
ATT_BLOCK = 128
ROPE_THETA = 10000.0
N_GROUPS = 4
EXPERTS_PER_GROUP = 8
N_EXPERTS = N_GROUPS * EXPERTS_PER_GROUP
TOP_K = 2
EXPERT_FF = 768
MOE_BLOCK = 128
EPS = 1e-6
IN_COLS = 4 * W_DELTA + 2 * H_DELTA + 3 * W_ATTN

kernel_name = 'hymba_gdn_dilated_swa_hier_moe'


def rmsnorm(x, g):
    xf = x.astype(jnp.float32)
    y = xf * lax.rsqrt(jnp.mean(xf * xf, -1, keepdims=True) + EPS)
    return (y * g.astype(jnp.float32)).astype(x.dtype)


def headnorm(x, g):
    return x * lax.rsqrt(jnp.mean(x * x, -1, keepdims=True) + EPS) * g.astype(jnp.float32)


def l2norm(x):
    return x * lax.rsqrt(jnp.sum(x * x, -1, keepdims=True) + EPS)


def rotary(x, pos):
    half = HEAD_DIM // 2
    inv = ROPE_THETA ** (-jnp.arange(half, dtype=jnp.float32) / half)
    ang = pos.astype(jnp.float32)[:, None] * inv[None, :]
    cos = jnp.cos(ang)[None, :, None, :]
    sin = jnp.sin(ang)[None, :, None, :]
    x1, x2 = x[..., :half], x[..., half:]
    return jnp.concatenate([x1 * cos - x2 * sin, x2 * cos + x1 * sin], -1)


def causal_conv(u, prev, w):
    upad = jnp.concatenate([prev.astype(u.dtype), u], 1)
    out = lax.conv_general_dilated(upad, w[:, None, :].astype(u.dtype), window_strides=(1,), padding='VALID',
                                   dimension_numbers=('NWC', 'WIO', 'NWC'), feature_group_count=u.shape[-1])
    return jax.nn.silu(out), upad[:, upad.shape[1] - (CONV_W - 1):]


def gated_delta_chunked(q, k, v, g, beta, S0):
    Bt, L, H, DK = q.shape
    C = min(CHUNK, L)
    pad = (-L) % C
    if pad:
        pz = lambda t: jnp.pad(t, [(0, 0), (0, pad)] + [(0, 0)] * (t.ndim - 2))
        q, k, v, g, beta = pz(q), pz(k), pz(v), pz(g), pz(beta)
    N = (L + pad) // C

    def to_chunks(t):
        t = t.reshape((Bt, N, C) + t.shape[2:])
        return t.transpose((1, 0, 3, 2) + tuple(range(4, t.ndim)))

    qc, kc, vc, gc, bc = to_chunks(q), to_chunks(k), to_chunks(v), to_chunks(g), to_chunks(beta)
    gam = jnp.cumsum(gc, -1)
    diff = gam[..., :, None] - gam[..., None, :]
    idx = jnp.arange(C)
    strict = idx[:, None] > idx[None, :]
    incl = idx[:, None] >= idx[None, :]
    dec_strict = jnp.exp(jnp.where(strict, diff, -jnp.inf))
    dec_incl = jnp.exp(jnp.where(incl, diff, -jnp.inf))
    a_mat = bc[..., :, None] * jnp.einsum('nbhck,nbhjk->nbhcj', kc, kc) * dec_strict
    ia = a_mat + jnp.eye(C, dtype=jnp.float32)
    eg = jnp.exp(gam)
    u_c = lax.linalg.triangular_solve(ia, bc[..., None] * vc, left_side=True, lower=True, unit_diagonal=True)
    w_c = lax.linalg.triangular_solve(ia, (bc * eg)[..., None] * kc, left_side=True, lower=True, unit_diagonal=True)
    qk = jnp.einsum('nbhck,nbhjk->nbhcj', qc, kc) * dec_incl
    kdec = kc * jnp.exp(gam[..., -1:] - gam)[..., None]
    eg_last = eg[..., -1]

    def step(S, xs):
        u_n, w_n, q_n, qk_n, eg_n, kdec_n, egl_n = xs
        delta = u_n - jnp.einsum('bhck,bhvk->bhcv', w_n, S)
        o = eg_n[..., None] * jnp.einsum('bhck,bhvk->bhcv', q_n, S) + jnp.einsum('bhcj,bhjv->bhcv', qk_n, delta)
        S = egl_n[..., None, None] * S + jnp.einsum('bhcv,bhck->bhvk', delta, kdec_n)
        return S, o

    S, o = lax.scan(step, S0, (u_c, w_c, qc, qk, eg, kdec, eg_last))
    o = o.transpose(1, 0, 3, 2, 4).reshape(Bt, N * C, H, v.shape[-1])[:, :L]
    return o, S


def delta_mixer(zq, zk, zv, zb, za, zg, conv_prev, S0, conv_w, A_log, dt_bias, o_norm_g):
    Bt, L, _ = zq.shape
    f32 = jnp.float32
    u = jnp.concatenate([zq, zk, zv], -1)
    c, conv_new = causal_conv(u, conv_prev, conv_w)
    c = c.astype(f32).reshape(Bt, L, 3, H_DELTA, HEAD_DIM)
    q = l2norm(c[:, :, 0]) * HEAD_DIM ** -0.5
    k = l2norm(c[:, :, 1])
    v = c[:, :, 2]
    beta = jax.nn.sigmoid(zb.astype(f32))
    g = -jnp.exp(A_log.astype(f32)) * jax.nn.softplus(za.astype(f32) + dt_bias.astype(f32))
    o, S = gated_delta_chunked(q, k, v, g, beta, S0.astype(f32))
    gate = jax.nn.silu(zg.astype(f32)).reshape(Bt, L, H_DELTA, HEAD_DIM)
    o = headnorm(o, o_norm_g) * gate
    return o.reshape(Bt, L, W_DELTA).astype(zq.dtype), conv_new, S.astype(zq.dtype)


def attn_qkv(aq, ak, av, pos, q_norm_g, k_norm_g):
    Bt, L, _ = aq.shape
    shp = (Bt, L, H_ATTN, HEAD_DIM)
    q = rotary(headnorm(aq.astype(jnp.float32).reshape(shp), q_norm_g), pos)
    k = rotary(headnorm(ak.astype(jnp.float32).reshape(shp), k_norm_g), pos)
    v = av.astype(jnp.float32).reshape(shp)
    return q, k, v


def dilated_branch_prompt(q, k, v, window, dil):
    Bt, S, H, hd = q.shape
    n = window // dil
    L = S // dil
    nb = -(-L // ATT_BLOCK)
    Lp = nb * ATT_BLOCK

    def sub(t):
        return t.reshape(Bt, L, dil, H, hd).transpose(0, 2, 1, 3, 4).reshape(Bt * dil, L, H, hd)

    qs = jnp.pad(sub(q), ((0, 0), (0, Lp - L), (0, 0), (0, 0))).reshape(Bt * dil, nb, ATT_BLOCK, H, hd)

    def band(t):
        tp = jnp.pad(sub(t), ((0, 0), (ATT_BLOCK, Lp - L), (0, 0), (0, 0))).reshape(Bt * dil, nb + 1, ATT_BLOCK, H, hd)
        return jnp.concatenate([tp[:, :-1], tp[:, 1:]], 2)

    kb, vb = band(k), band(v)
    s = jnp.einsum('znqhd,znkhd->znhqk', qs, kb) * HEAD_DIM ** -0.5
    i = jnp.arange(ATT_BLOCK)
    j = jnp.arange(2 * ATT_BLOCK)
    dist = i[:, None] + ATT_BLOCK - j[None, :]
    kpos = jnp.arange(nb)[:, None] * ATT_BLOCK - ATT_BLOCK + j[None, :]
    mask = ((dist >= 0) & (dist <= n))[None] & (kpos >= 0)[:, None, :]
    s = jnp.where(mask[None, :, None], s, -jnp.inf)
    m = jnp.max(s, -1)
    p = jnp.exp(s - m[..., None])
    l = jnp.sum(p, -1)
    o = jnp.einsum('znhqk,znkhd->znqhd', p, vb)

    def unsub(t):
        t = t.reshape((Bt, dil, Lp) + t.shape[3:])[:, :, :L]
        t = t.transpose((0, 2, 1) + tuple(range(3, t.ndim)))
        return t.reshape((Bt, S) + t.shape[3:])

    return unsub(m.transpose(0, 1, 3, 2)), unsub(l.transpose(0, 1, 3, 2)), unsub(o)


def dilated_branch_sample(q, k_all, v_all, wbuf, window, dil):
    T = q.shape[1]
    n = window // dil
    idx = wbuf + jnp.arange(T)[:, None] - jnp.arange(n + 1)[None, :] * dil
    valid = idx >= 0
    idx = jnp.maximum(idx, 0)
    kg = jnp.take(k_all, idx, axis=1)
    vg = jnp.take(v_all, idx, axis=1)
    s = jnp.einsum('bthd,btjhd->bthj', q, kg) * HEAD_DIM ** -0.5
    s = jnp.where(valid[None, :, None, :], s, -jnp.inf)
    m = jnp.max(s, -1)
    p = jnp.exp(s - m[..., None])
    l = jnp.sum(p, -1)
    o = jnp.einsum('bthj,btjhd->bthd', p, vg)
    return m, l, o


def merge_by_denominators(parts):
    ms = jnp.stack([p[0] for p in parts])
    ls = jnp.stack([p[1] for p in parts])
    os_ = jnp.stack([p[2] for p in parts])
    w = jnp.exp(ms - jnp.max(ms, 0, keepdims=True))
    return jnp.sum(w[..., None] * os_, 0) / jnp.sum(w * ls, 0)[..., None]


def moe_ffn(xn, w_group, b_group, w_router, b_router, w_g, w_u, w_d):
    T, D = xn.shape
    f32 = jnp.float32
    p_group = jax.nn.softmax((xn @ w_group).astype(f32) + b_group.astype(f32), -1)
    pg_top, g_top = lax.top_k(p_group, 1)
    logits_e = ((xn @ w_router).astype(f32) + b_router.astype(f32)).reshape(T, N_GROUPS, EXPERTS_PER_GROUP)
    p_in = jax.nn.softmax(logits_e[jnp.arange(T), g_top[:, 0]], -1)
    pe_top, e_top = lax.top_k(p_in, TOP_K)
    gate = pg_top * pe_top / jnp.sum(pe_top, -1, keepdims=True)
    expert = g_top * EXPERTS_PER_GROUP + e_top
    A = T * TOP_K
    flat_e = expert.reshape(A)
    flat_tok = jnp.repeat(jnp.arange(T, dtype=jnp.int32), TOP_K)
    flat_gate = gate.reshape(A)
    order = jnp.argsort(flat_e)
    se, stok, sgate = flat_e[order], flat_tok[order], flat_gate[order]
    counts = jnp.bincount(flat_e, length=N_EXPERTS)
    starts = jnp.cumsum(counts) - counts
    pcounts = (counts + MOE_BLOCK - 1) // MOE_BLOCK * MOE_BLOCK
    pends = jnp.cumsum(pcounts)
    pstarts = pends - pcounts
    dest = pstarts[se] + jnp.arange(A) - starts[se]
    NB = -(-A // MOE_BLOCK) + N_EXPERTS
    rows_tok = jnp.full((NB * MOE_BLOCK,), T, jnp.int32).at[dest].set(stok)
    block_e = jnp.minimum(jnp.searchsorted(pends, jnp.arange(NB) * MOE_BLOCK, side='right'), N_EXPERTS - 1)
    x_ext = jnp.concatenate([xn, jnp.zeros((1, D), xn.dtype)], 0)
    xb = x_ext[rows_tok].reshape(NB, MOE_BLOCK, D)

    def expert_block(args):
        xblk, e = args
        h = jax.nn.silu(xblk @ w_g[e]) * (xblk @ w_u[e])
        return h @ w_d[e]

    yb = lax.map(expert_block, (xb, block_e)).reshape(NB * MOE_BLOCK, D)
    y = jnp.zeros((T, D), f32).at[stok].add(yb[dest].astype(f32) * sgate[:, None])
    return y.astype(xn.dtype)


def trunk_layer(x, pos, conv_prev, S0, win_k, win_v, norm1_g, w_in, conv_w, A_log, dt_bias, o_norm_g,
                q_norm_g, k_norm_g, w_out, norm2_g, w_group, b_group, w_router, b_router,
                w_exp_gate, w_exp_up, w_exp_down):
    Bt, L, D = x.shape
    xn = rmsnorm(x, norm1_g)
    sizes = [W_DELTA, W_DELTA, W_DELTA, H_DELTA, H_DELTA, W_DELTA, W_ATTN, W_ATTN, W_ATTN]
    cuts = np.cumsum(sizes)[:-1].tolist()
    zq, zk, zv, zb, za, zg, aq, ak, av = jnp.split(xn @ w_in, cuts, axis=-1)
    o_a, conv_new, S_new = delta_mixer(zq, zk, zv, zb, za, zg, conv_prev, S0, conv_w, A_log, dt_bias, o_norm_g)
    q, k, v = attn_qkv(aq, ak, av, pos, q_norm_g, k_norm_g)
    if win_k is None:
        parts = [dilated_branch_prompt(q, k, v, w, d) for (w, d) in DIL_PAIRS]
        keep = min(WIN_MAX, L)
        wk_new, wv_new = k[:, L - keep:], v[:, L - keep:]
    else:
        wbuf = win_k.shape[1]
        k_all = jnp.concatenate([win_k.astype(jnp.float32), k], 1)
        v_all = jnp.concatenate([win_v.astype(jnp.float32), v], 1)
        parts = [dilated_branch_sample(q, k_all, v_all, wbuf, w, d) for (w, d) in DIL_PAIRS]
        wk_new, wv_new = k_all[:, k_all.shape[1] - wbuf:], v_all[:, v_all.shape[1] - wbuf:]
    o_b = merge_by_denominators(parts).reshape(Bt, L, W_ATTN).astype(x.dtype)
    h = x + jnp.concatenate([o_a, o_b], -1) @ w_out
    hn = rmsnorm(h, norm2_g)
    y = h + moe_ffn(hn.reshape(Bt * L, D), w_group, b_group, w_router, b_router,
                    w_exp_gate, w_exp_up, w_exp_down).reshape(Bt, L, D)
    return y, conv_new, S_new, wk_new.astype(x.dtype), wv_new.astype(x.dtype)


def setup_inputs(seed: int = 0) -> dict:
    key = jax.random.key(seed)
    ks = jax.random.split(key, 24)
    f32 = jnp.float32
    nrm = lambda k, shape, s: jax.random.normal(k, shape, f32) * s
    wbuf = min(WIN_MAX, PAST_LEN)
    dt = jnp.exp(jax.random.uniform(ks[9], (DEPTH, H_DELTA), f32, np.log(1e-3), np.log(1e-1)))
    return {
        'x_prompt': nrm(ks[0], (BATCH, SEQ, D_MODEL), 1.0),
        'x_sample': nrm(ks[1], (DEC_BATCH, DEC_SEQ, D_MODEL), 1.0),
        'state_conv': nrm(ks[2], (DEPTH, DEC_BATCH, CONV_W - 1, 3 * W_DELTA), 1.0),
        'state_delta': nrm(ks[3], (DEPTH, DEC_BATCH, H_DELTA, HEAD_DIM, HEAD_DIM), HEAD_DIM ** -0.5),
        'cache_win_k': nrm(ks[4], (DEPTH, DEC_BATCH, wbuf, H_ATTN, HEAD_DIM), 1.0),
        'cache_win_v': nrm(ks[5], (DEPTH, DEC_BATCH, wbuf, H_ATTN, HEAD_DIM), 1.0),
        'norm1_g': 1.0 + nrm(ks[6], (DEPTH, D_MODEL), 0.02),
        'w_in': nrm(ks[7], (DEPTH, D_MODEL, IN_COLS), D_MODEL ** -0.5),
        'conv_w': nrm(ks[8], (DEPTH, CONV_W, 3 * W_DELTA), CONV_W ** -0.5),
        'A_log': jnp.log(jax.random.uniform(ks[10], (DEPTH, H_DELTA), f32, 1.0, 16.0)),
        'dt_bias': dt + jnp.log(-jnp.expm1(-dt)),
        'o_norm_g': 1.0 + nrm(ks[11], (DEPTH, HEAD_DIM), 0.02),
        'q_norm_g': 1.0 + nrm(ks[12], (DEPTH, HEAD_DIM), 0.02),
        'k_norm_g': 1.0 + nrm(ks[13], (DEPTH, HEAD_DIM), 0.02),
        'w_out': nrm(ks[14], (DEPTH, MIX_WIDTH, D_MODEL), MIX_WIDTH ** -0.5),
        'norm2_g': 1.0 + nrm(ks[15], (DEPTH, D_MODEL), 0.02),
        'w_group': nrm(ks[16], (DEPTH, D_MODEL, N_GROUPS), D_MODEL ** -0.5),
        'b_group': nrm(ks[17], (DEPTH, N_GROUPS), 0.01),
        'w_router': nrm(ks[18], (DEPTH, D_MODEL, N_EXPERTS), D_MODEL ** -0.5),
        'b_router': nrm(ks[19], (DEPTH, N_EXPERTS), 0.01),
        'w_exp_gate': nrm(ks[20], (DEPTH, N_EXPERTS, D_MODEL, EXPERT_FF), D_MODEL ** -0.5),
        'w_exp_up': nrm(ks[21], (DEPTH, N_EXPERTS, D_MODEL, EXPERT_FF), D_MODEL ** -0.5),
        'w_exp_down': nrm(ks[22], (DEPTH, N_EXPERTS, EXPERT_FF, D_MODEL), EXPERT_FF ** -0.5),
    }


def reference(x_prompt, x_sample, state_conv, state_delta, cache_win_k, cache_win_v, norm1_g, w_in, conv_w,
              A_log, dt_bias, o_norm_g, q_norm_g, k_norm_g, w_out, norm2_g, w_group, b_group, w_router,
              b_router, w_exp_gate, w_exp_up, w_exp_down):
    Bp, Lp_, _ = x_prompt.shape
    pos_p = jnp.arange(Lp_, dtype=jnp.int32)
    pos_s = PAST_LEN + jnp.arange(x_sample.shape[1], dtype=jnp.int32)
    yp, ys = x_prompt, x_sample
    conv_p, conv_s, dl_p, dl_s, wk_p, wk_s, wv_p, wv_s = [], [], [], [], [], [], [], []
    for l in range(DEPTH):
        w = (norm1_g[l], w_in[l], conv_w[l], A_log[l], dt_bias[l], o_norm_g[l], q_norm_g[l], k_norm_g[l],
             w_out[l], norm2_g[l], w_group[l], b_group[l], w_router[l], b_router[l],
             w_exp_gate[l], w_exp_up[l], w_exp_down[l])
        conv0 = jnp.zeros((Bp, CONV_W - 1, 3 * W_DELTA), yp.dtype)
        S0 = jnp.zeros((Bp, H_DELTA, HEAD_DIM, HEAD_DIM), jnp.float32)
        yp, c1, s1, k1, v1 = trunk_layer(yp, pos_p, conv0, S0, None, None, *w)
        ys, c2, s2, k2, v2 = trunk_layer(ys, pos_s, state_conv[l], state_delta[l], cache_win_k[l], cache_win_v[l], *w)
        conv_p.append(c1); dl_p.append(s1); wk_p.append(k1); wv_p.append(v1)
        conv_s.append(c2); dl_s.append(s2); wk_s.append(k2); wv_s.append(v2)
    return (yp, ys, jnp.stack(conv_p), jnp.stack(conv_s), jnp.stack(dl_p), jnp.stack(dl_s),
            jnp.stack(wk_p), jnp.stack(wk_s), jnp.stack(wv_p), jnp.stack(wv_s))
```

```python
import functools
import math

import numpy as np
import jax
import jax.numpy as jnp
from jax import lax
from jax.experimental import pallas as pl
from jax.experimental.pallas import tpu as pltpu

F32 = jnp.float32
BF16 = jnp.bfloat16
HIGHEST = lax.Precision.HIGHEST

D_MODEL = 2048
HEAD_DIM = 128
H_DELTA = 8
H_ATTN = 8
W_DELTA = H_DELTA * HEAD_DIM
W_ATTN = H_ATTN * HEAD_DIM
CONV_W = 4
CHUNK = 64
DIL_PAIRS = ((128, 1), (512, 4), (2048, 16))
ATT_BLOCK = 128
ROPE_THETA = 10000.0
N_GROUPS = 4
EXPERTS_PER_GROUP = 8
N_EXPERTS = N_GROUPS * EXPERTS_PER_GROUP
TOP_K = 2
EXPERT_FF = 768
EPS = 1e-6
PAST_LEN = 16384

LANES = 128
SUBLANES = 8
V7X_VMEM_BYTES = 64 * 1024 * 1024
COMPILER_TEMP_BYTES = 12 * 1024 * 1024

COL_ZQ, COL_ZK, COL_ZV, COL_ZG, COL_AQ, COL_AK, COL_AV = (i * H_DELTA for i in range(7))
MAIN_COLS = 7 * W_DELTA
ATTN_SCALE = HEAD_DIM ** -0.5


def _vmem_limit(block_bytes, scratch_bytes=0):
    return int(min(2 * block_bytes + scratch_bytes + COMPILER_TEMP_BYTES, V7X_VMEM_BYTES - 4 * 1024 * 1024))


def _nbytes(shape, dtype):
    return int(np.prod(shape)) * jnp.dtype(dtype).itemsize


def _rms(x, g):
    return x * lax.rsqrt(jnp.mean(x * x, -1, keepdims=True) + EPS) * g


def _sigmoid(x):
    return 1.0 / (1.0 + jnp.exp(-x))


def _dot_nt(a, b, **kw):
    return lax.dot_general(a, b, (((1,), (1,)), ((), ())), **kw)


def _dot_tn(a, b, **kw):
    return lax.dot_general(a, b, (((0,), (0,)), ((), ())), **kw)


def _in_proj_kernel(x_ref, g_ref, w_ref, ws_ref, z_ref, zs_ref, xn_ref):
    @pl.when(pl.program_id(1) == 0)
    def _():
        xn_ref[...] = _rms(x_ref[...], g_ref[...]).astype(BF16)
        zs_ref[...] = jnp.dot(xn_ref[...], ws_ref[...], preferred_element_type=F32)

    z_ref[...] = jnp.dot(xn_ref[...], w_ref[...], preferred_element_type=F32)


def _in_proj(x2d, g, w_main, w_small, tm, tn=1024):
    T = x2d.shape[0]
    blocks = (_nbytes((tm, D_MODEL), F32) + _nbytes((D_MODEL, tn), BF16) + _nbytes((D_MODEL, LANES), BF16)
              + _nbytes((tm, tn), F32) + _nbytes((tm, LANES), F32))
    return pl.pallas_call(
        _in_proj_kernel,
        grid=(T // tm, MAIN_COLS // tn),
        in_specs=[
            pl.BlockSpec((tm, D_MODEL), lambda i, j: (i, 0)),
            pl.BlockSpec((1, D_MODEL), lambda i, j: (0, 0)),
            pl.BlockSpec((D_MODEL, tn), lambda i, j: (0, j)),
            pl.BlockSpec((D_MODEL, LANES), lambda i, j: (0, 0)),
        ],
        out_specs=[
            pl.BlockSpec((tm, tn), lambda i, j: (i, j)),
            pl.BlockSpec((tm, LANES), lambda i, j: (i, 0)),
        ],
        out_shape=[jax.ShapeDtypeStruct((T, MAIN_COLS), F32), jax.ShapeDtypeStruct((T, LANES), F32)],
        scratch_shapes=[pltpu.VMEM((tm, D_MODEL), BF16)],
        compiler_params=pltpu.CompilerParams(
            dimension_semantics=("parallel", "arbitrary"),
            vmem_limit_bytes=_vmem_limit(blocks, _nbytes((tm, D_MODEL), BF16))),
        name="in_proj",
    )(x2d, g, w_main, w_small)


def _delta_chunk(cq, ck, cv, beta_b, g_b, S):
    C = cq.shape[0]
    hp = dict(precision=HIGHEST, preferred_element_type=F32)
    q = cq * lax.rsqrt(jnp.sum(cq * cq, -1, keepdims=True) + EPS) * ATTN_SCALE
    k = ck * lax.rsqrt(jnp.sum(ck * ck, -1, keepdims=True) + EPS)
    row = lax.broadcasted_iota(jnp.int32, (C, C), 0)
    col = lax.broadcasted_iota(jnp.int32, (C, C), 1)
    eye = (row == col).astype(F32)
    gam_b = jnp.dot((row >= col).astype(F32), g_b, **hp)
    gam_col = gam_b[:, :C]
    gam_row = jnp.dot(jnp.ones((C, C), F32), eye * gam_col, **hp)
    dec = jnp.exp(jnp.minimum(gam_col - gam_row, 0.0))
    dec_strict = jnp.where(row > col, dec, 0.0)
    dec_incl = jnp.where(row >= col, dec, 0.0)
    kq = _dot_nt(jnp.concatenate([k, q], 0), k, **hp)
    qk = kq[C:] * dec_incl
    x = -(beta_b[:, :C] * kq[:C] * dec_strict)
    inv = eye + x
    for _ in range(int(math.log2(C)) - 1):
        x = jnp.dot(x, x, **hp)
        inv = inv + jnp.dot(inv, x, **hp)
    eg = jnp.exp(gam_b)
    uw = jnp.dot(inv, jnp.concatenate([beta_b * cv, beta_b * eg * k], 1), **hp)
    u, w = uw[:, :HEAD_DIM], uw[:, HEAD_DIM:]
    gam_last = gam_b[C - 1:C, :]
    kdec = k * jnp.exp(gam_last - gam_b)
    wq_s = _dot_nt(jnp.concatenate([w, q], 0), S, **hp)
    delta = u - wq_s[:C]
    o = eg * wq_s[C:] + jnp.dot(qk, delta, **hp)
    s_new = jnp.exp(gam_last) * S + _dot_tn(delta, kdec, **hp)
    return o, s_new


def _delta_kernel(*refs, lt, chunk, has_state):
    if has_state:
        (zq_ref, zk_ref, zv_ref, zg_ref, zs_ref, cw_ref, a_ref, dt_ref, og_ref, prev_ref, s0_ref,
         o_ref, sout_ref, cout_ref, u_s, c_s, s_s) = refs
    else:
        (zq_ref, zk_ref, zv_ref, zg_ref, zs_ref, cw_ref, a_ref, dt_ref, og_ref,
         o_ref, sout_ref, cout_ref, u_s, c_s, s_s) = refs
    l = pl.program_id(1)
    halo = SUBLANES
    width = 3 * W_DELTA

    @pl.when(l == 0)
    def _():
        if has_state:
            u_s[0:halo, :] = jnp.zeros((halo, width), F32)
            u_s[halo - (CONV_W - 1):halo, :] = prev_ref[0]
            s_s[...] = s0_ref[0]
        else:
            u_s[0:halo, :] = jnp.zeros((halo, width), F32)
            s_s[...] = jnp.zeros(s_s.shape, F32)

    @pl.when(l > 0)
    def _():
        u_s[0:halo, :] = u_s[lt:lt + halo, :]

    u_s[halo:halo + lt, 0:W_DELTA] = zq_ref[0]
    u_s[halo:halo + lt, W_DELTA:2 * W_DELTA] = zk_ref[0]
    u_s[halo:halo + lt, 2 * W_DELTA:width] = zv_ref[0]

    base = halo - (CONV_W - 1)
    acc = cw_ref[0:1, :] * u_s[base:base + lt, :]
    for i in range(1, CONV_W):
        acc = acc + cw_ref[i:i + 1, :] * u_s[base + i:base + i + lt, :]
    c_s[...] = acc * _sigmoid(acc)

    neg_a = -jnp.exp(a_ref[...])

    def chunk_body(ci, carry):
        r0 = pl.multiple_of(ci * chunk, chunk)
        zs = zs_ref[0, pl.ds(r0, chunk), :]
        beta_all = _sigmoid(zs)
        sp_in = zs + dt_ref[...]
        g_all = neg_a * (jnp.maximum(sp_in, 0.0) + jnp.log1p(jnp.exp(-jnp.abs(sp_in))))
        for h in range(H_DELTA):
            lo = h * HEAD_DIM
            cq = c_s[pl.ds(r0, chunk), lo:lo + HEAD_DIM]
            ck = c_s[pl.ds(r0, chunk), W_DELTA + lo:W_DELTA + lo + HEAD_DIM]
            cv = c_s[pl.ds(r0, chunk), 2 * W_DELTA + lo:2 * W_DELTA + lo + HEAD_DIM]
            beta_b = jnp.broadcast_to(beta_all[:, h:h + 1], (chunk, HEAD_DIM))
            g_b = jnp.broadcast_to(g_all[:, H_DELTA + h:H_DELTA + h + 1], (chunk, HEAD_DIM))
            o, s_new = _delta_chunk(cq, ck, cv, beta_b, g_b, s_s[h])
            s_s[h] = s_new
            zg = zg_ref[0, pl.ds(r0, chunk), lo:lo + HEAD_DIM]
            o = _rms(o, og_ref[...]) * (zg * _sigmoid(zg))
            o_ref[0, pl.ds(r0, chunk), lo:lo + HEAD_DIM] = o.astype(o_ref.dtype)
        return carry

    lax.fori_loop(0, lt // chunk, chunk_body, 0)

    @pl.when(l == pl.num_programs(1) - 1)
    def _():
        sout_ref[0] = s_s[...]
        cout_ref[0] = u_s[halo + lt - (CONV_W - 1):halo + lt, :]


def _delta_mixer(z3, zs3, conv_w, a_vec, dt_vec, o_norm_g, lt, chunk, conv_prev=None, s0=None):
    B, L, _ = z3.shape
    has_state = conv_prev is not None
    width = 3 * W_DELTA
    col_blk = lambda c: pl.BlockSpec((1, lt, W_DELTA), lambda b, l, c=c: (b, l, c))
    full = lambda shape: pl.BlockSpec(shape, lambda b, l: (0,) * len(shape))
    in_specs = [col_blk(0), col_blk(1), col_blk(2), col_blk(3),
                pl.BlockSpec((1, lt, LANES), lambda b, l: (b, l, 0)),
                full((CONV_W, width)), full((1, LANES)), full((1, LANES)), full((1, HEAD_DIM))]
    args = [z3, z3, z3, z3, zs3, conv_w, a_vec, dt_vec, o_norm_g]
    if has_state:
        in_specs += [pl.BlockSpec((1, CONV_W - 1, width), lambda b, l: (b, 0, 0)),
                     pl.BlockSpec((1, H_DELTA, HEAD_DIM, HEAD_DIM), lambda b, l: (b, 0, 0, 0))]
        args += [conv_prev, s0]
    blocks = (4 * _nbytes((lt, W_DELTA), F32) + _nbytes((lt, LANES), F32) + _nbytes((CONV_W, width), F32)
              + _nbytes((lt, W_DELTA), BF16) + 2 * _nbytes((H_DELTA, HEAD_DIM, HEAD_DIM), F32)
              + 2 * _nbytes((SUBLANES, width), F32))
    scratch = [pltpu.VMEM((SUBLANES + lt, width), F32), pltpu.VMEM((lt, width), F32),
               pltpu.VMEM((H_DELTA, HEAD_DIM, HEAD_DIM), F32)]
    scratch_bytes = (_nbytes((SUBLANES + lt, width), F32) + _nbytes((lt, width), F32)
                     + _nbytes((H_DELTA, HEAD_DIM, HEAD_DIM), F32))
    return pl.pallas_call(
        functools.partial(_delta_kernel, lt=lt, chunk=chunk, has_state=has_state),
        grid=(B, L // lt),
        in_specs=in_specs,
        out_specs=[
            pl.BlockSpec((1, lt, W_DELTA), lambda b, l: (b, l, 0)),
            pl.BlockSpec((1, H_DELTA, HEAD_DIM, HEAD_DIM), lambda b, l: (b, 0, 0, 0)),
            pl.BlockSpec((1, CONV_W - 1, width), lambda b, l: (b, 0, 0)),
        ],
        out_shape=[jax.ShapeDtypeStruct((B, L, W_DELTA), BF16),
                   jax.ShapeDtypeStruct((B, H_DELTA, HEAD_DIM, HEAD_DIM), F32),
                   jax.ShapeDtypeStruct((B, CONV_W - 1, width), F32)],
        scratch_shapes=scratch,
        compiler_params=pltpu.CompilerParams(
            dimension_semantics=("parallel", "arbitrary"),
            vmem_limit_bytes=_vmem_limit(blocks, scratch_bytes)),
        name="delta_mixer",
    )(*args)


def _norm_rot(x, g, cos2, sin2):
    xn = _rms(x, g)
    return xn * cos2 + pltpu.roll(xn, HEAD_DIM // 2, 1) * sin2


def _softmax_parts(s, mask, vb):
    s = jnp.where(mask, s, -jnp.inf)
    m = jnp.max(s, -1, keepdims=True)
    p = jnp.exp(s - m)
    l = jnp.sum(p, -1, keepdims=True)
    o = jnp.dot(p.astype(BF16), vb.astype(BF16), preferred_element_type=F32)
    return m, l, o


def _merge(parts):
    m_all = functools.reduce(jnp.maximum, [p[0] for p in parts])
    ws = [jnp.exp(p[0] - m_all) for p in parts]
    num = sum(w * p[2] for w, p in zip(ws, parts))
    den = sum(w * p[1] for w, p in zip(ws, parts))
    return num / den


def _attn_prompt_kernel(aq_ref, ak_ref, av_ref, cos_ref, sin_ref, qg_ref, kg_ref,
                        ob_ref, kout_ref, vout_ref, q_s, k_s, o_s, m_s, l_s):
    L = q_s.shape[0]
    blk = ATT_BLOCK
    rows = 256

    def prep(i, carry):
        r = pl.multiple_of(i * rows, rows)
        sl = pl.ds(r, rows)
        cos2, sin2 = cos_ref[sl, :], sin_ref[sl, :]
        q_s[sl, :] = _norm_rot(aq_ref[0, sl, :], qg_ref[...], cos2, sin2)
        k = _norm_rot(ak_ref[0, sl, :], kg_ref[...], cos2, sin2)
        k_s[sl, :] = k
        kout_ref[0, sl, :] = k
        vout_ref[0, sl, :] = av_ref[0, sl, :]
        return carry

    lax.fori_loop(0, L // rows, prep, 0)

    qi = lax.broadcasted_iota(jnp.int32, (blk, 2 * blk), 0)
    kj = lax.broadcasted_iota(jnp.int32, (blk, 2 * blk), 1)
    band_mask = (kj >= qi) & (kj <= qi + blk)
    causal_mask = (lax.broadcasted_iota(jnp.int32, (blk, blk), 1)
                   <= lax.broadcasted_iota(jnp.int32, (blk, blk), 0))

    def block(branch, dil, q_start, k_start, nk, mask):
        stride = dil if dil > 1 else None
        q = q_s[pl.ds(q_start, blk, stride=stride), :]
        kb = k_s[pl.ds(k_start, nk, stride=stride), :]
        vb = av_ref[0, pl.ds(k_start, nk, stride=stride), :]
        s = _dot_nt(q.astype(BF16), kb.astype(BF16), preferred_element_type=F32) * ATTN_SCALE
        m, l, o = _softmax_parts(s, mask, vb)
        dst = pl.ds(q_start, blk, stride=stride)
        o_s[branch, dst, :] = o
        m_s[branch, dst, :] = jnp.broadcast_to(m, (blk, HEAD_DIM))
        l_s[branch, dst, :] = jnp.broadcast_to(l, (blk, HEAD_DIM))

    for branch, (window, dil) in enumerate(DIL_PAIRS):
        assert window // dil == blk
        nb = L // dil // blk

        def residue(r, carry, branch=branch, dil=dil, nb=nb):
            block(branch, dil, r, r, blk, causal_mask)

            def later(i, c):
                q_start = r + i * (dil * blk)
                block(branch, dil, q_start, q_start - dil * blk, 2 * blk, band_mask)
                return c

            if nb > 1:
                lax.fori_loop(1, nb, later, 0)
            return carry

        lax.fori_loop(0, dil, residue, 0)

    def merge(i, carry):
        sl = pl.ds(pl.multiple_of(i * rows, rows), rows)
        parts = [(m_s[b, sl, :], l_s[b, sl, :], o_s[b, sl, :]) for b in range(len(DIL_PAIRS))]
        ob_ref[0, sl, :] = _merge(parts).astype(ob_ref.dtype)
        return carry

    lax.fori_loop(0, L // rows, merge, 0)


def _attn_prompt(z3, cos2, sin2, q_norm_g, k_norm_g):
    B, L, _ = z3.shape
    nbr = len(DIL_PAIRS)
    head_blk = lambda c0: pl.BlockSpec((1, L, HEAD_DIM), lambda b, h, c0=c0: (b, 0, c0 + h))
    out_blk = pl.BlockSpec((1, L, HEAD_DIM), lambda b, h: (b, 0, h))
    tab = pl.BlockSpec((L, HEAD_DIM), lambda b, h: (0, 0))
    vec = pl.BlockSpec((1, HEAD_DIM), lambda b, h: (0, 0))
    blocks = 5 * _nbytes((L, HEAD_DIM), F32) + 2 * _nbytes((L, HEAD_DIM), F32) + _nbytes((L, HEAD_DIM), BF16)
    scratch_bytes = (2 + 3 * nbr) * _nbytes((L, HEAD_DIM), F32)
    return pl.pallas_call(
        _attn_prompt_kernel,
        grid=(B, H_ATTN),
        in_specs=[head_blk(COL_AQ), head_blk(COL_AK), head_blk(COL_AV), tab, tab, vec, vec],
        out_specs=[out_blk, out_blk, out_blk],
        out_shape=[jax.ShapeDtypeStruct((B, L, W_ATTN), BF16),
                   jax.ShapeDtypeStruct((B, L, W_ATTN), F32),
                   jax.ShapeDtypeStruct((B, L, W_ATTN), F32)],
        scratch_shapes=[pltpu.VMEM((L, HEAD_DIM), F32), pltpu.VMEM((L, HEAD_DIM), F32),
                        pltpu.VMEM((nbr, L, HEAD_DIM), F32), pltpu.VMEM((nbr, L, HEAD_DIM), F32),
                        pltpu.VMEM((nbr, L, HEAD_DIM), F32)],
        compiler_params=pltpu.CompilerParams(
            dimension_semantics=("parallel", "parallel"),
            vmem_limit_bytes=_vmem_limit(blocks, scratch_bytes)),
        name="attn_prompt",
    )(z3, z3, z3, cos2, sin2, q_norm_g, k_norm_g)


HEADS_PER_STEP = 2


def _attn_sample_kernel(aq_ref, ak_ref, av_ref, ck_ref, cv_ref, cos_ref, sin_ref, qg_ref, kg_ref,
                        ob_ref, wk_ref, wv_ref):
    T = aq_ref.shape[1]
    wbuf = ck_ref.shape[1]
    cos2, sin2 = cos_ref[...], sin_ref[...]
    t_c = lax.broadcasted_iota(jnp.int32, (T, wbuf), 0)
    i_c = lax.broadcasted_iota(jnp.int32, (T, wbuf), 1)
    dist_c = wbuf + t_c - i_c
    t_n = lax.broadcasted_iota(jnp.int32, (T, T), 0)
    dist_n = t_n - lax.broadcasted_iota(jnp.int32, (T, T), 1)
    for hh in range(HEADS_PER_STEP):
        lanes = slice(hh * HEAD_DIM, (hh + 1) * HEAD_DIM)
        q = _norm_rot(aq_ref[0, :, lanes], qg_ref[...], cos2, sin2)
        k_new = _norm_rot(ak_ref[0, :, lanes], kg_ref[...], cos2, sin2)
        v_new = av_ref[0, :, lanes]
        k_c = ck_ref[0, :, lanes]
        v_c = cv_ref[0, :, lanes]
        qb = q.astype(BF16)
        s_c = _dot_nt(qb, k_c.astype(BF16), preferred_element_type=F32) * ATTN_SCALE
        s_n = _dot_nt(qb, k_new.astype(BF16), preferred_element_type=F32) * ATTN_SCALE
        parts = []
        for window, dil in DIL_PAIRS:
            mask_c = ((dist_c & (dil - 1)) == 0) & (dist_c <= window)
            mask_n = (dist_n >= 0) & ((dist_n & (dil - 1)) == 0) & (dist_n <= window)
            sc = jnp.where(mask_c, s_c, -jnp.inf)
            sn = jnp.where(mask_n, s_n, -jnp.inf)
            m = jnp.maximum(jnp.max(sc, -1, keepdims=True), jnp.max(sn, -1, keepdims=True))
            pc = jnp.exp(sc - m)
            pn = jnp.exp(sn - m)
            l = jnp.sum(pc, -1, keepdims=True) + jnp.sum(pn, -1, keepdims=True)
            o = (jnp.dot(pc.astype(BF16), v_c.astype(BF16), preferred_element_type=F32)
                 + jnp.dot(pn.astype(BF16), v_new.astype(BF16), preferred_element_type=F32))
            parts.append((m, l, o))
        ob_ref[0, :, lanes] = _merge(parts).astype(ob_ref.dtype)
        wk_ref[0, wbuf - T:wbuf, lanes] = k_new
        wv_ref[0, wbuf - T:wbuf, lanes] = v_new
    wk_ref[0, 0:wbuf - T, :] = ck_ref[0, T:wbuf, :]
    wv_ref[0, 0:wbuf - T, :] = cv_ref[0, T:wbuf, :]


def _attn_sample(z3, cache_k, cache_v, cos2, sin2, q_norm_g, k_norm_g):
    B, T, _ = z3.shape
    wbuf = cache_k.shape[1]
    wl = HEADS_PER_STEP * HEAD_DIM
    per = W_ATTN // wl
    zblk = lambda c0: pl.BlockSpec((1, T, wl), lambda b, g, c0=c0: (b, 0, c0 // HEADS_PER_STEP + g))
    cblk = pl.BlockSpec((1, wbuf, wl), lambda b, g: (b, 0, g))
    tab = pl.BlockSpec((T, HEAD_DIM), lambda b, g: (0, 0))
    vec = pl.BlockSpec((1, HEAD_DIM), lambda b, g: (0, 0))
    blocks = 4 * _nbytes((wbuf, wl), F32) + 4 * _nbytes((T, wl), F32)
    return pl.pallas_call(
        _attn_sample_kernel,
        grid=(B, per),
        in_specs=[zblk(COL_AQ), zblk(COL_AK), zblk(COL_AV), cblk, cblk, tab, tab, vec, vec],
        out_specs=[pl.BlockSpec((1, T, wl), lambda b, g: (b, 0, g)), cblk, cblk],
        out_shape=[jax.ShapeDtypeStruct((B, T, W_ATTN), BF16),
                   jax.ShapeDtypeStruct((B, wbuf, W_ATTN), F32),
                   jax.ShapeDtypeStruct((B, wbuf, W_ATTN), F32)],
        compiler_params=pltpu.CompilerParams(
            dimension_semantics=("parallel", "parallel"),
            vmem_limit_bytes=_vmem_limit(blocks)),
        name="attn_sample",
    )(z3, z3, z3, cache_k, cache_v, cos2, sin2, q_norm_g, k_norm_g)


def _out_proj_kernel(oa_ref, ob_ref, x_ref, wa_ref, wb_ref, g_ref, wr_ref, br_ref, h_ref, hn_ref, lg_ref):
    h = (x_ref[...] + jnp.dot(oa_ref[...], wa_ref[...], preferred_element_type=F32)
         + jnp.dot(ob_ref[...], wb_ref[...], preferred_element_type=F32))
    h_ref[...] = h
    hn = _rms(h, g_ref[...])
    hn_ref[...] = hn.astype(BF16)
    lg_ref[...] = jnp.dot(hn, wr_ref[...], precision=HIGHEST, preferred_element_type=F32) + br_ref[...]


def _out_proj(o_a, o_b, x2d, w_a, w_b, g, w_rt, b_rt, tm):
    T = x2d.shape[0]
    row = lambda w: pl.BlockSpec((tm, w), lambda i: (i, 0))
    full = lambda shape: pl.BlockSpec(shape, lambda i: (0, 0))
    blocks = (2 * _nbytes((tm, W_DELTA), BF16) + 2 * _nbytes((tm, D_MODEL), F32) + 2 * _nbytes((W_DELTA, D_MODEL), BF16)
              + _nbytes((D_MODEL, LANES), F32) + _nbytes((tm, D_MODEL), BF16) + _nbytes((tm, LANES), F32))
    return pl.pallas_call(
        _out_proj_kernel,
        grid=(T // tm,),
        in_specs=[row(W_DELTA), row(W_ATTN), row(D_MODEL), full((W_DELTA, D_MODEL)), full((W_ATTN, D_MODEL)),
                  full((1, D_MODEL)), full((D_MODEL, LANES)), full((1, LANES))],
        out_specs=[row(D_MODEL), row(D_MODEL), row(LANES)],
        out_shape=[jax.ShapeDtypeStruct((T, D_MODEL), F32), jax.ShapeDtypeStruct((T, D_MODEL), BF16),
                   jax.ShapeDtypeStruct((T, LANES), F32)],
        compiler_params=pltpu.CompilerParams(
            dimension_semantics=("parallel",), vmem_limit_bytes=_vmem_limit(blocks)),
        name="out_proj",
    )(o_a, o_b, x2d, w_a, w_b, g, w_rt, b_rt)


def _expert_kernel(be_ref, x_ref, wg_ref, wu_ref, wd_ref, y_ref):
    del be_ref
    x = x_ref[...]
    a = jnp.dot(x, wg_ref[0], preferred_element_type=F32)
    b = jnp.dot(x, wu_ref[0], preferred_element_type=F32)
    hmid = (a * _sigmoid(a)) * b
    y_ref[...] = jnp.dot(hmid.astype(BF16), wd_ref[0], preferred_element_type=F32)


def _experts(block_e, xb, w_g, w_u, w_d, tm):
    rows = xb.shape[0]
    blocks = (_nbytes((tm, D_MODEL), BF16) + 3 * _nbytes((D_MODEL, EXPERT_FF), BF16) + _nbytes((tm, D_MODEL), F32))
    return pl.pallas_call(
        _expert_kernel,
        grid_spec=pltpu.PrefetchScalarGridSpec(
            num_scalar_prefetch=1,
            grid=(rows // tm,),
            in_specs=[
                pl.BlockSpec((tm, D_MODEL), lambda i, be: (i, 0)),
                pl.BlockSpec((1, D_MODEL, EXPERT_FF), lambda i, be: (be[i], 0, 0)),
                pl.BlockSpec((1, D_MODEL, EXPERT_FF), lambda i, be: (be[i], 0, 0)),
                pl.BlockSpec((1, EXPERT_FF, D_MODEL), lambda i, be: (be[i], 0, 0)),
            ],
            out_specs=pl.BlockSpec((tm, D_MODEL), lambda i, be: (i, 0)),
        ),
        out_shape=jax.ShapeDtypeStruct((rows, D_MODEL), F32),
        compiler_params=pltpu.CompilerParams(
            dimension_semantics=("arbitrary",), vmem_limit_bytes=_vmem_limit(blocks)),
        name="experts",
    )(block_e, xb, w_g, w_u, w_d)


def _moe(h, hn, logits, w_g, w_u, w_d, tm):
    T = h.shape[0]
    p_group = jax.nn.softmax(logits[:, :N_GROUPS], -1)
    pg_top, g_top = lax.top_k(p_group, 1)
    logits_e = logits[:, N_GROUPS:N_GROUPS + N_EXPERTS].reshape(T, N_GROUPS, EXPERTS_PER_GROUP)
    p_in = jax.nn.softmax(logits_e[jnp.arange(T), g_top[:, 0]], -1)
    pe_top, e_top = lax.top_k(p_in, TOP_K)
    gate = pg_top * pe_top / jnp.sum(pe_top, -1, keepdims=True)
    expert = g_top * EXPERTS_PER_GROUP + e_top
    A = T * TOP_K
    flat_e = expert.reshape(A)
    flat_tok = jnp.repeat(jnp.arange(T, dtype=jnp.int32), TOP_K)
    flat_gate = gate.reshape(A)
    order = jnp.argsort(flat_e)
    se, stok, sgate = flat_e[order], flat_tok[order], flat_gate[order]
    counts = jnp.bincount(flat_e, length=N_EXPERTS)
    starts = jnp.cumsum(counts) - counts
    pcounts = (counts + tm - 1) // tm * tm
    pends = jnp.cumsum(pcounts)
    pstarts = pends - pcounts
    dest = pstarts[se] + jnp.arange(A) - starts[se]
    nb = -(-A // tm) + N_EXPERTS
    rows_tok = jnp.full((nb * tm,), T, jnp.int32).at[dest].set(stok)
    block_e = jnp.minimum(jnp.searchsorted(pends, jnp.arange(nb) * tm, side='right'), N_EXPERTS - 1).astype(jnp.int32)
    hn_ext = jnp.concatenate([hn, jnp.zeros((1, D_MODEL), hn.dtype)], 0)
    xb = hn_ext[rows_tok]
    yb = _experts(block_e, xb, w_g, w_u, w_d, tm)
    return h + jnp.zeros((T, D_MODEL), F32).at[stok].add(yb[dest] * sgate[:, None])


def _rope_tables(pos):
    half = HEAD_DIM // 2
    inv = ROPE_THETA ** (-jnp.arange(half, dtype=F32) / half)
    ang = pos.astype(F32)[:, None] * inv[None, :]
    cos, sin = jnp.cos(ang), jnp.sin(ang)
    return jnp.concatenate([cos, cos], -1), jnp.concatenate([-sin, sin], -1)


def _layer(x, pos, w, tm, lt, chunk, moe_tm, conv_prev=None, s0=None, win_k=None, win_v=None):
    B, L, D = x.shape
    T = B * L
    x2d = x.reshape(T, D)
    z, zs = _in_proj(x2d, w['norm1_g'], w['w_main'], w['w_small'], tm)
    z3 = z.reshape(B, L, MAIN_COLS)
    zs3 = zs.reshape(B, L, LANES)
    o_a, s_new, conv_new = _delta_mixer(z3, zs3, w['conv_w'], w['a_vec'], w['dt_vec'], w['o_norm_g'], lt, chunk,
                                        conv_prev, s0)
    cos2, sin2 = _rope_tables(pos)
    if win_k is None:
        o_b, wk_new, wv_new = _attn_prompt(z3, cos2, sin2, w['q_norm_g'], w['k_norm_g'])
    else:
        wbuf = win_k.shape[1]
        o_b, wk_new, wv_new = _attn_sample(z3, win_k.reshape(B, wbuf, W_ATTN), win_v.reshape(B, wbuf, W_ATTN),
                                           cos2, sin2, w['q_norm_g'], w['k_norm_g'])
    h, hn, logits = _out_proj(o_a.reshape(T, W_DELTA), o_b.reshape(T, W_ATTN), x2d, w['w_out_a'], w['w_out_b'],
                              w['norm2_g'], w['w_rt'], w['b_rt'], tm)
    y = _moe(h, hn, logits, w['w_g'], w['w_u'], w['w_d'], moe_tm)
    keep = wk_new.shape[1]
    return (y.reshape(B, L, D), conv_new, s_new,
            wk_new.reshape(B, keep, H_ATTN, HEAD_DIM), wv_new.reshape(B, keep, H_ATTN, HEAD_DIM))


def _prep_weights(norm1_g, w_in, conv_w, A_log, dt_bias, o_norm_g, q_norm_g, k_norm_g, w_out, norm2_g,
                  w_group, b_group, w_router, b_router, w_exp_gate, w_exp_up, w_exp_down):
    sizes = [W_DELTA, W_DELTA, W_DELTA, H_DELTA, H_DELTA, W_DELTA, W_ATTN, W_ATTN, W_ATTN]
    zq, zk, zv, zb, za, zg, aq, ak, av = jnp.split(w_in, np.cumsum(sizes)[:-1].tolist(), axis=-1)
    pad = LANES - 2 * H_DELTA
    lane_vec = lambda v: jnp.pad(v.astype(F32), (H_DELTA, LANES - 2 * H_DELTA))[None, :]
    return dict(
        norm1_g=norm1_g[None, :],
        w_main=jnp.concatenate([zq, zk, zv, zg, aq, ak, av], -1).astype(BF16),
        w_small=jnp.pad(jnp.concatenate([zb, za], -1), ((0, 0), (0, pad))).astype(BF16),
        conv_w=conv_w,
        a_vec=lane_vec(A_log), dt_vec=lane_vec(dt_bias),
        o_norm_g=o_norm_g[None, :], q_norm_g=q_norm_g[None, :], k_norm_g=k_norm_g[None, :],
        w_out_a=w_out[:W_DELTA].astype(BF16), w_out_b=w_out[W_DELTA:].astype(BF16),
        norm2_g=norm2_g[None, :],
        w_rt=jnp.pad(jnp.concatenate([w_group, w_router], -1), ((0, 0), (0, LANES - N_GROUPS - N_EXPERTS))),
        b_rt=jnp.pad(jnp.concatenate([b_group, b_router], -1), (0, LANES - N_GROUPS - N_EXPERTS))[None, :],
        w_g=w_exp_gate.astype(BF16), w_u=w_exp_up.astype(BF16), w_d=w_exp_down.astype(BF16),
    )


def kernel(x_prompt, x_sample, state_conv, state_delta, cache_win_k, cache_win_v, norm1_g, w_in, conv_w, A_log, dt_bias, o_norm_g, q_norm_g, k_norm_g, w_out, norm2_g, w_group, b_group, w_router, b_router, w_exp_gate, w_exp_up, w_exp_down):
    depth = w_in.shape[0]
    pos_p = jnp.arange(x_prompt.shape[1], dtype=jnp.int32)
    pos_s = PAST_LEN + jnp.arange(x_sample.shape[1], dtype=jnp.int32)
    yp, ys = x_prompt, x_sample
    outs = [[] for _ in range(8)]
    for l in range(depth):
        w = _prep_weights(norm1_g[l], w_in[l], conv_w[l], A_log[l], dt_bias[l], o_norm_g[l], q_norm_g[l],
                          k_norm_g[l], w_out[l], norm2_g[l], w_group[l], b_group[l], w_router[l], b_router[l],
                          w_exp_gate[l], w_exp_up[l], w_exp_down[l])
        yp, c1, s1, k1, v1 = _layer(yp, pos_p, w, tm=512, lt=256, chunk=CHUNK, moe_tm=512)
        ls = x_sample.shape[1]
        ys, c2, s2, k2, v2 = _layer(ys, pos_s, w, tm=x_sample.shape[0] * ls, lt=ls, chunk=min(CHUNK, ls), moe_tm=128,
                                    conv_prev=state_conv[l], s0=state_delta[l],
                                    win_k=cache_win_k[l], win_v=cache_win_v[l])
        for lst, v in zip(outs, (c1, c2, s1, s2, k1, k2, v1, v2)):
            lst.append(v)
    return (yp, ys) + tuple(jnp.stack(o) for o in outs)
```

```python
import functools
import math

import numpy as np
import jax
import jax.numpy as jnp
from jax import lax
from jax.experimental import pallas as pl
from jax.experimental.pallas import tpu as pltpu

F32 = jnp.float32
BF16 = jnp.bfloat16
HIGHEST = lax.Precision.HIGHEST

D_MODEL = 2048
HEAD_DIM = 128
H_DELTA = 8
H_ATTN = 8
W_DELTA = H_DELTA * HEAD_DIM
W_ATTN = H_ATTN * HEAD_DIM
CONV_W = 4
CHUNK = 64
DIL_PAIRS = ((128, 1), (512, 4), (2048, 16))
ATT_BLOCK = 128
ROPE_THETA = 10000.0
N_GROUPS = 4
EXPERTS_PER_GROUP = 8
N_EXPERTS = N_GROUPS * EXPERTS_PER_GROUP
TOP_K = 2
EXPERT_FF = 768
EPS = 1e-6
PAST_LEN = 16384

LANES = 128
SUBLANES = 8
V7X_VMEM_BYTES = 64 * 1024 * 1024
COMPILER_TEMP_BYTES = 12 * 1024 * 1024

COL_ZQ, COL_ZK, COL_ZV, COL_ZG, COL_AQ, COL_AK, COL_AV = (i * H_DELTA for i in range(7))
MAIN_COLS = 7 * W_DELTA
ATTN_SCALE = HEAD_DIM ** -0.5


def _vmem_limit(block_bytes, scratch_bytes=0):
    return int(min(2 * block_bytes + scratch_bytes + COMPILER_TEMP_BYTES, V7X_VMEM_BYTES - 4 * 1024 * 1024))


def _nbytes(shape, dtype):
    return int(np.prod(shape)) * jnp.dtype(dtype).itemsize


def _rms(x, g):
    return x * lax.rsqrt(jnp.mean(x * x, -1, keepdims=True) + EPS) * g


def _sigmoid(x):
    return 1.0 / (1.0 + jnp.exp(-x))


def _dot_nt(a, b, **kw):
    return lax.dot_general(a, b, (((1,), (1,)), ((), ())), **kw)


def _dot_tn(a, b, **kw):
    return lax.dot_general(a, b, (((0,), (0,)), ((), ())), **kw)


def _in_proj_kernel(x_ref, g_ref, w_ref, ws_ref, z_ref, zs_ref, xn_ref):
    @pl.when(pl.program_id(1) == 0)
    def _():
        xn_ref[...] = _rms(x_ref[...], g_ref[...]).astype(BF16)
        zs_ref[...] = jnp.dot(xn_ref[...], ws_ref[...], preferred_element_type=F32)

    z_ref[...] = jnp.dot(xn_ref[...], w_ref[...], preferred_element_type=F32)


def _in_proj(x2d, g, w_main, w_small, tm, tn=1024):
    T = x2d.shape[0]
    blocks = (_nbytes((tm, D_MODEL), F32) + _nbytes((D_MODEL, tn), BF16) + _nbytes((D_MODEL, LANES), BF16)
              + _nbytes((tm, tn), F32) + _nbytes((tm, LANES), F32))
    return pl.pallas_call(
        _in_proj_kernel,
        grid=(T // tm, MAIN_COLS // tn),
        in_specs=[
            pl.BlockSpec((tm, D_MODEL), lambda i, j: (i, 0)),
            pl.BlockSpec((1, D_MODEL), lambda i, j: (0, 0)),
            pl.BlockSpec((D_MODEL, tn), lambda i, j: (0, j)),
            pl.BlockSpec((D_MODEL, LANES), lambda i, j: (0, 0)),
        ],
        out_specs=[
            pl.BlockSpec((tm, tn), lambda i, j: (i, j)),
            pl.BlockSpec((tm, LANES), lambda i, j: (i, 0)),
        ],
        out_shape=[jax.ShapeDtypeStruct((T, MAIN_COLS), F32), jax.ShapeDtypeStruct((T, LANES), F32)],
        scratch_shapes=[pltpu.VMEM((tm, D_MODEL), BF16)],
        compiler_params=pltpu.CompilerParams(
            dimension_semantics=("parallel", "arbitrary"),
            vmem_limit_bytes=_vmem_limit(blocks, _nbytes((tm, D_MODEL), BF16))),
        name="in_proj",
    )(x2d, g, w_main, w_small)


def _delta_chunk(cq, ck, cv, beta_b, g_b, S):
    C = cq.shape[0]
    hp = dict(precision=HIGHEST, preferred_element_type=F32)
    q = cq * lax.rsqrt(jnp.sum(cq * cq, -1, keepdims=True) + EPS) * ATTN_SCALE
    k = ck * lax.rsqrt(jnp.sum(ck * ck, -1, keepdims=True) + EPS)
    row = lax.broadcasted_iota(jnp.int32, (C, C), 0)
    col = lax.broadcasted_iota(jnp.int32, (C, C), 1)
    eye = (row == col).astype(F32)
    gam_b = jnp.dot((row >= col).astype(F32), g_b, **hp)
    gam_col = gam_b[:, :C]
    gam_row = jnp.dot(jnp.ones((C, C), F32), eye * gam_col, **hp)
    dec = jnp.exp(jnp.minimum(gam_col - gam_row, 0.0))
    dec_strict = jnp.where(row > col, dec, 0.0)
    dec_incl = jnp.where(row >= col, dec, 0.0)
    kq = _dot_nt(jnp.concatenate([k, q], 0), k, **hp)
    qk = kq[C:] * dec_incl
    x = -(beta_b[:, :C] * kq[:C] * dec_strict)
    inv = eye + x
    for _ in range(int(math.log2(C)) - 1):
        x = jnp.dot(x, x, **hp)
        inv = inv + jnp.dot(inv, x, **hp)
    eg = jnp.exp(gam_b)
    uw = jnp.dot(inv, jnp.concatenate([beta_b * cv, beta_b * eg * k], 1), **hp)
    u, w = uw[:, :HEAD_DIM], uw[:, HEAD_DIM:]
    gam_last = gam_b[C - 1:C, :]
    kdec = k * jnp.exp(gam_last - gam_b)
    wq_s = _dot_nt(jnp.concatenate([w, q], 0), S, **hp)
    delta = u - wq_s[:C]
    o = eg * wq_s[C:] + jnp.dot(qk, delta, **hp)
    s_new = jnp.exp(gam_last) * S + _dot_tn(delta, kdec, **hp)
    return o, s_new


def _delta_kernel(*refs, lt, chunk, has_state):
    if has_state:
        (zq_ref, zk_ref, zv_ref, zg_ref, zs_ref, cw_ref, a_ref, dt_ref, og_ref, prev_ref, s0_ref,
         o_ref, sout_ref, cout_ref, u_s, c_s, s_s) = refs
    else:
        (zq_ref, zk_ref, zv_ref, zg_ref, zs_ref, cw_ref, a_ref, dt_ref, og_ref,
         o_ref, sout_ref, cout_ref, u_s, c_s, s_s) = refs
    l = pl.program_id(1)
    halo = SUBLANES
    width = 3 * W_DELTA

    @pl.when(l == 0)
    def _():
        if has_state:
            u_s[0:halo, :] = jnp.zeros((halo, width), F32)
            u_s[halo - (CONV_W - 1):halo, :] = prev_ref[0]
            s_s[...] = s0_ref[0]
        else:
            u_s[0:halo, :] = jnp.zeros((halo, width), F32)
            s_s[...] = jnp.zeros(s_s.shape, F32)

    @pl.when(l > 0)
    def _():
        u_s[0:halo, :] = u_s[lt:lt + halo, :]

    u_s[halo:halo + lt, 0:W_DELTA] = zq_ref[0]
    u_s[halo:halo + lt, W_DELTA:2 * W_DELTA] = zk_ref[0]
    u_s[halo:halo + lt, 2 * W_DELTA:width] = zv_ref[0]

    base = halo - (CONV_W - 1)
    acc = cw_ref[0:1, :] * u_s[base:base + lt, :]
    for i in range(1, CONV_W):
        acc = acc + cw_ref[i:i + 1, :] * u_s[base + i:base + i + lt, :]
    c_s[...] = acc * _sigmoid(acc)

    neg_a = -jnp.exp(a_ref[...])

    def chunk_body(ci, carry):
        r0 = pl.multiple_of(ci * chunk, chunk)
        zs = zs_ref[0, pl.ds(r0, chunk), :]
        beta_all = _sigmoid(zs)
        sp_in = zs + dt_ref[...]
        g_all = neg_a * (jnp.maximum(sp_in, 0.0) + jnp.log1p(jnp.exp(-jnp.abs(sp_in))))
        for h in range(H_DELTA):
            lo = h * HEAD_DIM
            cq = c_s[pl.ds(r0, chunk), lo:lo + HEAD_DIM]
            ck = c_s[pl.ds(r0, chunk), W_DELTA + lo:W_DELTA + lo + HEAD_DIM]
            cv = c_s[pl.ds(r0, chunk), 2 * W_DELTA + lo:2 * W_DELTA + lo + HEAD_DIM]
            beta_b = jnp.broadcast_to(beta_all[:, h:h + 1], (chunk, HEAD_DIM))
            g_b = jnp.broadcast_to(g_all[:, H_DELTA + h:H_DELTA + h + 1], (chunk, HEAD_DIM))
            o, s_new = _delta_chunk(cq, ck, cv, beta_b, g_b, s_s[h])
            s_s[h] = s_new
            zg = zg_ref[0, pl.ds(r0, chunk), lo:lo + HEAD_DIM]
            o = _rms(o, og_ref[...]) * (zg * _sigmoid(zg))
            o_ref[0, pl.ds(r0, chunk), lo:lo + HEAD_DIM] = o.astype(o_ref.dtype)
        return carry

    lax.fori_loop(0, lt // chunk, chunk_body, 0)

    @pl.when(l == pl.num_programs(1) - 1)
    def _():
        sout_ref[0] = s_s[...]
        cout_ref[0] = u_s[halo + lt - (CONV_W - 1):halo + lt, :]


def _delta_mixer(z3, zs3, conv_w, a_vec, dt_vec, o_norm_g, lt, chunk, conv_prev=None, s0=None):
    B, L, _ = z3.shape
    has_state = conv_prev is not None
    width = 3 * W_DELTA
    col_blk = lambda c: pl.BlockSpec((1, lt, W_DELTA), lambda b, l, c=c: (b, l, c))
    full = lambda shape: pl.BlockSpec(shape, lambda b, l: (0,) * len(shape))
    in_specs = [col_blk(0), col_blk(1), col_blk(2), col_blk(3),
                pl.BlockSpec((1, lt, LANES), lambda b, l: (b, l, 0)),
                full((CONV_W, width)), full((1, LANES)), full((1, LANES)), full((1, HEAD_DIM))]
    args = [z3, z3, z3, z3, zs3, conv_w, a_vec, dt_vec, o_norm_g]
    if has_state:
        in_specs += [pl.BlockSpec((1, CONV_W - 1, width), lambda b, l: (b, 0, 0)),
                     pl.BlockSpec((1, H_DELTA, HEAD_DIM, HEAD_DIM), lambda b, l: (b, 0, 0, 0))]
        args += [conv_prev, s0]
    blocks = (4 * _nbytes((lt, W_DELTA), F32) + _nbytes((lt, LANES), F32) + _nbytes((CONV_W, width), F32)
              + _nbytes((lt, W_DELTA), BF16) + 2 * _nbytes((H_DELTA, HEAD_DIM, HEAD_DIM), F32)
              + 2 * _nbytes((SUBLANES, width), F32))
    scratch = [pltpu.VMEM((SUBLANES + lt, width), F32), pltpu.VMEM((lt, width), F32),
               pltpu.VMEM((H_DELTA, HEAD_DIM, HEAD_DIM), F32)]
    scratch_bytes = (_nbytes((SUBLANES + lt, width), F32) + _nbytes((lt, width), F32)
                     + _nbytes((H_DELTA, HEAD_DIM, HEAD_DIM), F32))
    return pl.pallas_call(
        functools.partial(_delta_kernel, lt=lt, chunk=chunk, has_state=has_state),
        grid=(B, L // lt),
        in_specs=in_specs,
        out_specs=[
            pl.BlockSpec((1, lt, W_DELTA), lambda b, l: (b, l, 0)),
            pl.BlockSpec((1, H_DELTA, HEAD_DIM, HEAD_DIM), lambda b, l: (b, 0, 0, 0)),
            pl.BlockSpec((1, CONV_W - 1, width), lambda b, l: (b, 0, 0)),
        ],
        out_shape=[jax.ShapeDtypeStruct((B, L, W_DELTA), BF16),
                   jax.ShapeDtypeStruct((B, H_DELTA, HEAD_DIM, HEAD_DIM), F32),
                   jax.ShapeDtypeStruct((B, CONV_W - 1, width), F32)],
        scratch_shapes=scratch,
        compiler_params=pltpu.CompilerParams(
            dimension_semantics=("parallel", "arbitrary"),
            vmem_limit_bytes=_vmem_limit(blocks, scratch_bytes)),
        name="delta_mixer",
    )(*args)


def _split_bf16(x):
    hi = x.astype(BF16)
    return hi, (x - hi.astype(F32)).astype(BF16)


def _dot3(a, b, dot=jnp.dot):
    ah, al = _split_bf16(a)
    bh, bl = _split_bf16(b)
    kw = dict(preferred_element_type=F32)
    return dot(ah, bh, **kw) + dot(ah, bl, **kw) + dot(al, bh, **kw)


def _beta_g(zs, neg_a, dt):
    sp_in = zs + dt
    return _sigmoid(zs), neg_a * (jnp.maximum(sp_in, 0.0) + jnp.log1p(jnp.exp(-jnp.abs(sp_in))))


def _delta_prompt_kernel(zq_ref, zk_ref, zv_ref, zg_ref, zs_ref, cw_ref, a_ref, dt_ref, og_ref,
                         o_ref, sout_ref, cout_ref,
                         u_s, c_s, s_s, uu_s, ww_s, qe_s, kd_s, qk_s, egl_s, *, lt):
    C = CHUNK
    P2 = 2 * C
    n_pairs = H_DELTA // 2
    l = pl.program_id(1)
    halo = SUBLANES
    width = 3 * W_DELTA

    @pl.when(l == 0)
    def _():
        u_s[0:halo, :] = jnp.zeros((halo, width), F32)
        s_s[...] = jnp.zeros(s_s.shape, F32)

    @pl.when(l > 0)
    def _():
        u_s[0:halo, :] = u_s[lt:lt + halo, :]

    u_s[halo:halo + lt, 0:W_DELTA] = zq_ref[0]
    u_s[halo:halo + lt, W_DELTA:2 * W_DELTA] = zk_ref[0]
    u_s[halo:halo + lt, 2 * W_DELTA:width] = zv_ref[0]

    base = halo - (CONV_W - 1)
    acc = cw_ref[0:1, :] * u_s[base:base + lt, :]
    for i in range(1, CONV_W):
        acc = acc + cw_ref[i:i + 1, :] * u_s[base + i:base + i + lt, :]
    c_s[...] = acc * _sigmoid(acc)

    neg_a = -jnp.exp(a_ref[...])
    row = lax.broadcasted_iota(jnp.int32, (P2, P2), 0)
    col = lax.broadcasted_iota(jnp.int32, (P2, P2), 1)
    same_head = (row & C) == (col & C)
    m_strict = jnp.where(same_head & (row > col), 1.0, 0.0)
    m_incl = jnp.where(same_head & (row >= col), 1.0, 0.0)
    eye = jnp.where(row == col, 1.0, 0.0)
    tri = jnp.where(lax.broadcasted_iota(jnp.int32, (C, C), 0) >= lax.broadcasted_iota(jnp.int32, (C, C), 1),
                    1.0, 0.0).astype(BF16)
    lane = lax.broadcasted_iota(jnp.int32, (1, P2), 1)
    head_lanes = lambda h: slice(h * HEAD_DIM, (h + 1) * HEAD_DIM)
    pairs = range(n_pairs)

    def phase_a(ci, carry):
        rows = pl.ds(pl.multiple_of(ci * C, C), C)
        beta_all, g_all = _beta_g(zs_ref[0, rows, :], neg_a, dt_ref[...])
        g1 = g_all.astype(BF16)
        r1 = g_all - g1.astype(F32)
        g2 = r1.astype(BF16)
        g3 = (r1 - g2.astype(F32)).astype(BF16)
        gam_all = (jnp.dot(tri, g1, preferred_element_type=F32) + jnp.dot(tri, g2, preferred_element_type=F32)
                   + jnp.dot(tri, g3, preferred_element_type=F32))
        gam_t = jnp.concatenate([gam_all, gam_all], 0).T

        k2, q2, rhs2, beta2, gcol, grow, qe, kdec = [], [], [], [], [], [], [], []
        for p in pairs:
            ks, qs, vs, bs, gs = [], [], [], [], []
            for h in (2 * p, 2 * p + 1):
                cq = c_s[rows, head_lanes(h)]
                ck = c_s[rows, W_DELTA + h * HEAD_DIM:W_DELTA + (h + 1) * HEAD_DIM]
                cv = c_s[rows, 2 * W_DELTA + h * HEAD_DIM:2 * W_DELTA + (h + 1) * HEAD_DIM]
                q = cq * lax.rsqrt(jnp.sum(cq * cq, -1, keepdims=True) + EPS) * ATTN_SCALE
                k = ck * lax.rsqrt(jnp.sum(ck * ck, -1, keepdims=True) + EPS)
                gam_b = jnp.broadcast_to(gam_all[:, H_DELTA + h:H_DELTA + h + 1], (C, HEAD_DIM))
                gam_last = gam_b[C - 1:C, :]
                kd_s[rows, head_lanes(h)] = k * jnp.exp(gam_last - gam_b)
                egl_s[pl.ds(ci * H_DELTA + h, 1), :] = jnp.exp(gam_last)
                ks.append(k)
                qs.append(q)
                vs.append(cv)
                bs.append(jnp.broadcast_to(beta_all[:, h:h + 1], (C, HEAD_DIM)))
                gs.append(gam_b)
            k2.append(jnp.concatenate(ks, 0))
            q2.append(jnp.concatenate(qs, 0))
            beta2.append(jnp.concatenate(bs, 0))
            gcol.append(jnp.concatenate(gs, 0))
            grow.append(jnp.where(lane < C, gam_t[H_DELTA + 2 * p:H_DELTA + 2 * p + 1, :],
                                  gam_t[H_DELTA + 2 * p + 1:H_DELTA + 2 * p + 2, :]))
            eg = jnp.exp(gcol[p])
            qe.append(eg * q2[p])
            rhs2.append(jnp.concatenate([beta2[p] * jnp.concatenate(vs, 0), beta2[p] * eg * k2[p]], 1))

        kq = [_dot3(jnp.concatenate([k2[p], q2[p]], 0), k2[p], dot=_dot_nt) for p in pairs]
        dec = [jnp.exp(jnp.minimum(gcol[p] - grow[p], 0.0)) for p in pairs]
        for p in pairs:
            qk_s[ci * n_pairs + p] = kq[p][P2:] * (dec[p] * m_incl)
        xs = [-(beta2[p] * kq[p][:P2] * (dec[p] * m_strict)) for p in pairs]
        invs = [eye + x for x in xs]
        xs = [_dot3(x, x) for x in xs]
        for _ in range(int(math.log2(C)) - 2):
            ys = [_dot3(jnp.concatenate([inv, x], 0), x) for inv, x in zip(invs, xs)]
            invs = [inv + y[:P2] for inv, y in zip(invs, ys)]
            xs = [y[P2:] for y in ys]
        invs = [inv + _dot3(inv, x) for inv, x in zip(invs, xs)]
        uw = [_dot3(invs[p], rhs2[p]) for p in pairs]
        for p in pairs:
            for hh in range(2):
                h = 2 * p + hh
                uu_s[rows, head_lanes(h)] = uw[p][hh * C:(hh + 1) * C, :HEAD_DIM]
                ww_s[rows, head_lanes(h)] = uw[p][hh * C:(hh + 1) * C, HEAD_DIM:]
                qe_s[rows, head_lanes(h)] = qe[p][hh * C:(hh + 1) * C, :]
        return carry

    lax.fori_loop(0, lt // C, phase_a, 0)

    def phase_b(ci, carry):
        rows = pl.ds(pl.multiple_of(ci * C, C), C)
        heads = range(H_DELTA)
        wq_s = [_dot3(jnp.concatenate([ww_s[rows, head_lanes(h)], qe_s[rows, head_lanes(h)]], 0), s_s[h], dot=_dot_nt)
                for h in heads]
        delta = [uu_s[rows, head_lanes(h)] - wq_s[h][:C] for h in heads]
        od = [_dot3(qk_s[ci * n_pairs + p], jnp.concatenate([delta[2 * p], delta[2 * p + 1]], 0)) for p in pairs]
        for h in heads:
            o = wq_s[h][C:] + od[h // 2][(h % 2) * C:(h % 2 + 1) * C]
            zg = zg_ref[0, rows, head_lanes(h)]
            o_ref[0, rows, head_lanes(h)] = (_rms(o, og_ref[...]) * (zg * _sigmoid(zg))).astype(o_ref.dtype)
        for h in heads:
            s_s[h] = egl_s[pl.ds(ci * H_DELTA + h, 1), :] * s_s[h] + _dot3(delta[h].T, kd_s[rows, head_lanes(h)])
        return carry

    lax.fori_loop(0, lt // C, phase_b, 0)

    @pl.when(l == pl.num_programs(1) - 1)
    def _():
        sout_ref[0] = s_s[...]
        cout_ref[0] = u_s[halo + lt - (CONV_W - 1):halo + lt, :]


def _delta_prompt(z3, zs3, conv_w, a_vec, dt_vec, o_norm_g, lt):
    B, L, _ = z3.shape
    width = 3 * W_DELTA
    n_chunks = lt // CHUNK
    col_blk = lambda c: pl.BlockSpec((1, lt, W_DELTA), lambda b, l, c=c: (b, l, c))
    full = lambda shape: pl.BlockSpec(shape, lambda b, l: (0,) * len(shape))
    blocks = (4 * _nbytes((lt, W_DELTA), F32) + _nbytes((lt, LANES), F32) + _nbytes((CONV_W, width), F32)
              + _nbytes((lt, W_DELTA), BF16) + _nbytes((H_DELTA, HEAD_DIM, HEAD_DIM), F32)
              + _nbytes((SUBLANES, width), F32))
    scratch_dims = [(SUBLANES + lt, width), (lt, width), (H_DELTA, HEAD_DIM, HEAD_DIM),
                    (lt, W_DELTA), (lt, W_DELTA), (lt, W_DELTA), (lt, W_DELTA),
                    (n_chunks * H_DELTA // 2, 2 * CHUNK, 2 * CHUNK), (n_chunks * H_DELTA, HEAD_DIM)]
    return pl.pallas_call(
        functools.partial(_delta_prompt_kernel, lt=lt),
        grid=(B, L // lt),
        in_specs=[col_blk(0), col_blk(1), col_blk(2), col_blk(3),
                  pl.BlockSpec((1, lt, LANES), lambda b, l: (b, l, 0)),
                  full((CONV_W, width)), full((1, LANES)), full((1, LANES)), full((1, HEAD_DIM))],
        out_specs=[
            pl.BlockSpec((1, lt, W_DELTA), lambda b, l: (b, l, 0)),
            pl.BlockSpec((1, H_DELTA, HEAD_DIM, HEAD_DIM), lambda b, l: (b, 0, 0, 0)),
            pl.BlockSpec((1, CONV_W - 1, width), lambda b, l: (b, 0, 0)),
        ],
        out_shape=[jax.ShapeDtypeStruct((B, L, W_DELTA), BF16),
                   jax.ShapeDtypeStruct((B, H_DELTA, HEAD_DIM, HEAD_DIM), F32),
                   jax.ShapeDtypeStruct((B, CONV_W - 1, width), F32)],
        scratch_shapes=[pltpu.VMEM(d, F32) for d in scratch_dims],
        compiler_params=pltpu.CompilerParams(
            dimension_semantics=("parallel", "arbitrary"),
            vmem_limit_bytes=_vmem_limit(blocks, sum(_nbytes(d, F32) for d in scratch_dims))),
        name="delta_prompt",
    )(z3, z3, z3, z3, zs3, conv_w, a_vec, dt_vec, o_norm_g)


def _norm_rot(x, g, cos2, sin2):
    xn = _rms(x, g)
    return xn * cos2 + pltpu.roll(xn, HEAD_DIM // 2, 1) * sin2


def _softmax_parts(s, mask, vb):
    s = jnp.where(mask, s, -jnp.inf)
    m = jnp.max(s, -1, keepdims=True)
    p = jnp.exp(s - m)
    l = jnp.sum(p, -1, keepdims=True)
    o = jnp.dot(p.astype(BF16), vb.astype(BF16), preferred_element_type=F32)
    return m, l, o


def _merge(parts):
    m_all = functools.reduce(jnp.maximum, [p[0] for p in parts])
    ws = [jnp.exp(p[0] - m_all) for p in parts]
    num = sum(w * p[2] for w, p in zip(ws, parts))
    den = sum(w * p[1] for w, p in zip(ws, parts))
    return num / den


def _attn_prompt_kernel(aq_ref, ak_ref, av_ref, cos_ref, sin_ref, qg_ref, kg_ref,
                        ob_ref, kout_ref, vout_ref, q_s, k_s, o_s, m_s, l_s):
    L = q_s.shape[0]
    blk = ATT_BLOCK
    rows = 256

    def prep(i, carry):
        r = pl.multiple_of(i * rows, rows)
        sl = pl.ds(r, rows)
        cos2, sin2 = cos_ref[sl, :], sin_ref[sl, :]
        q_s[sl, :] = _norm_rot(aq_ref[0, sl, :], qg_ref[...], cos2, sin2)
        k = _norm_rot(ak_ref[0, sl, :], kg_ref[...], cos2, sin2)
        k_s[sl, :] = k
        kout_ref[0, sl, :] = k
        vout_ref[0, sl, :] = av_ref[0, sl, :]
        return carry

    lax.fori_loop(0, L // rows, prep, 0)

    qi = lax.broadcasted_iota(jnp.int32, (blk, 2 * blk), 0)
    kj = lax.broadcasted_iota(jnp.int32, (blk, 2 * blk), 1)
    band_mask = (kj >= qi) & (kj <= qi + blk)
    causal_mask = (lax.broadcasted_iota(jnp.int32, (blk, blk), 1)
                   <= lax.broadcasted_iota(jnp.int32, (blk, blk), 0))

    def block(branch, dil, q_start, k_start, nk, mask):
        stride = dil if dil > 1 else None
        q = q_s[pl.ds(q_start, blk, stride=stride), :]
        kb = k_s[pl.ds(k_start, nk, stride=stride), :]
        vb = av_ref.at[0][pl.ds(k_start, nk, stride=stride), :]
        s = _dot_nt(q.astype(BF16), kb.astype(BF16), preferred_element_type=F32) * ATTN_SCALE
        m, l, o = _softmax_parts(s, mask, vb)
        dst = pl.ds(q_start, blk, stride=stride)
        o_s.at[branch][dst, :] = o
        m_s.at[branch][dst, :] = jnp.broadcast_to(m, (blk, HEAD_DIM))
        l_s.at[branch][dst, :] = jnp.broadcast_to(l, (blk, HEAD_DIM))

    for branch, (window, dil) in enumerate(DIL_PAIRS):
        assert window // dil == blk
        nb = L // dil // blk

        def residue(r, carry, branch=branch, dil=dil, nb=nb):
            block(branch, dil, r, r, blk, causal_mask)

            def later(i, c):
                q_start = r + i * (dil * blk)
                block(branch, dil, q_start, q_start - dil * blk, 2 * blk, band_mask)
                return c

            if nb > 1:
                lax.fori_loop(1, nb, later, 0)
            return carry

        lax.fori_loop(0, dil, residue, 0)

    def merge(i, carry):
        sl = pl.ds(pl.multiple_of(i * rows, rows), rows)
        parts = [(m_s[b, sl, :], l_s[b, sl, :], o_s[b, sl, :]) for b in range(len(DIL_PAIRS))]
        ob_ref[0, sl, :] = _merge(parts).astype(ob_ref.dtype)
        return carry

    lax.fori_loop(0, L // rows, merge, 0)


def _attn_prompt(z3, cos2, sin2, q_norm_g, k_norm_g):
    B, L, _ = z3.shape
    nbr = len(DIL_PAIRS)
    head_blk = lambda c0: pl.BlockSpec((1, L, HEAD_DIM), lambda b, h, c0=c0: (b, 0, c0 + h))
    out_blk = pl.BlockSpec((1, L, HEAD_DIM), lambda b, h: (b, 0, h))
    tab = pl.BlockSpec((L, HEAD_DIM), lambda b, h: (0, 0))
    vec = pl.BlockSpec((1, HEAD_DIM), lambda b, h: (0, 0))
    blocks = 5 * _nbytes((L, HEAD_DIM), F32) + 2 * _nbytes((L, HEAD_DIM), F32) + _nbytes((L, HEAD_DIM), BF16)
    scratch_bytes = (2 + 3 * nbr) * _nbytes((L, HEAD_DIM), F32)
    return pl.pallas_call(
        _attn_prompt_kernel,
        grid=(B, H_ATTN),
        in_specs=[head_blk(COL_AQ), head_blk(COL_AK), head_blk(COL_AV), tab, tab, vec, vec],
        out_specs=[out_blk, out_blk, out_blk],
        out_shape=[jax.ShapeDtypeStruct((B, L, W_ATTN), BF16),
                   jax.ShapeDtypeStruct((B, L, W_ATTN), F32),
                   jax.ShapeDtypeStruct((B, L, W_ATTN), F32)],
        scratch_shapes=[pltpu.VMEM((L, HEAD_DIM), F32), pltpu.VMEM((L, HEAD_DIM), F32),
                        pltpu.VMEM((nbr, L, HEAD_DIM), F32), pltpu.VMEM((nbr, L, HEAD_DIM), F32),
                        pltpu.VMEM((nbr, L, HEAD_DIM), F32)],
        compiler_params=pltpu.CompilerParams(
            dimension_semantics=("parallel", "parallel"),
            vmem_limit_bytes=_vmem_limit(blocks, scratch_bytes)),
        name="attn_prompt",
    )(z3, z3, z3, cos2, sin2, q_norm_g, k_norm_g)


def _attn_sample_kernel(aq_ref, ak_ref, av_ref, ck_ref, cv_ref, ckh_ref, cvh_ref, cos_ref, sin_ref, qg_ref, kg_ref,
                        ob_ref, wk_ref, wv_ref, q_s, kn_s, m_s, l_s, acc_s):
    T = aq_ref.shape[1]
    H = H_ATTN
    rows = ck_ref.shape[1]
    P = rows // H
    j = pl.program_id(1)
    nj = pl.num_programs(1)
    wbuf = P * nj
    shift = T * H
    nbr = len(DIL_PAIRS)
    head_lanes = lambda h: slice(h * HEAD_DIM, (h + 1) * HEAD_DIM)

    @pl.when(j == 0)
    def _():
        cos2, sin2 = cos_ref[...], sin_ref[...]
        for h in range(H):
            q_s[h] = _norm_rot(aq_ref[0, :, head_lanes(h)], qg_ref[...], cos2, sin2)
            kn_s[h] = _norm_rot(ak_ref[0, :, head_lanes(h)], kg_ref[...], cos2, sin2)
        m_s[...] = jnp.full(m_s.shape, -jnp.inf, F32)
        l_s[...] = jnp.zeros(l_s.shape, F32)
        acc_s[...] = jnp.zeros(acc_s.shape, F32)

    def update(br, h, s, mask, v):
        sm = jnp.where(mask, s, -jnp.inf)
        m_old = m_s[br, h]
        m_new = jnp.maximum(m_old, jnp.max(sm, -1, keepdims=True))
        m_safe = jnp.where(m_new == -jnp.inf, 0.0, m_new)
        p = jnp.exp(sm - m_safe)
        alpha = jnp.exp(m_old - m_safe)
        l_s[br, h] = alpha * l_s[br, h] + jnp.sum(p, -1, keepdims=True)
        acc_s[br, h] = alpha * acc_s[br, h] + jnp.dot(p.astype(BF16), v.astype(BF16), preferred_element_type=F32)
        m_s[br, h] = m_new

    t_c = lax.broadcasted_iota(jnp.int32, (T, P), 0)
    dist_c = wbuf + t_c - (j * P + lax.broadcasted_iota(jnp.int32, (T, P), 1))
    for h in range(H):
        k_h = ck_ref.at[0][pl.ds(h, P, stride=H), :]
        v_h = cv_ref.at[0][pl.ds(h, P, stride=H), :]
        s = _dot_nt(q_s[h].astype(BF16), k_h.astype(BF16), preferred_element_type=F32) * ATTN_SCALE
        for br, (window, dil) in enumerate(DIL_PAIRS):
            update(br, h, s, ((dist_c & (dil - 1)) == 0) & (dist_c <= window), v_h)

    wk_ref[0, 0:rows - shift, :] = ck_ref[0, shift:rows, :]
    wv_ref[0, 0:rows - shift, :] = cv_ref[0, shift:rows, :]

    @pl.when(j < nj - 1)
    def _():
        wk_ref[0, rows - shift:rows, :] = ckh_ref[0]
        wv_ref[0, rows - shift:rows, :] = cvh_ref[0]

    @pl.when(j == nj - 1)
    def _():
        t_n = lax.broadcasted_iota(jnp.int32, (T, T), 0)
        dist_n = t_n - lax.broadcasted_iota(jnp.int32, (T, T), 1)
        for h in range(H):
            k_new = kn_s[h]
            v_new = av_ref[0, :, head_lanes(h)]
            s = _dot_nt(q_s[h].astype(BF16), k_new.astype(BF16), preferred_element_type=F32) * ATTN_SCALE
            for br, (window, dil) in enumerate(DIL_PAIRS):
                update(br, h, s, (dist_n >= 0) & ((dist_n & (dil - 1)) == 0) & (dist_n <= window), v_new)
            parts = [(m_s[br, h], l_s[br, h], acc_s[br, h]) for br in range(nbr)]
            ob_ref[0, :, head_lanes(h)] = _merge(parts).astype(ob_ref.dtype)
            wk_ref.at[0][pl.ds(rows - shift + h, T, stride=H), :] = k_new
            wv_ref.at[0][pl.ds(rows - shift + h, T, stride=H), :] = v_new


def _attn_sample(z3, cache_k, cache_v, cos2, sin2, q_norm_g, k_norm_g, pos_chunk=512):
    B, T, _ = z3.shape
    rows_all = cache_k.shape[1]
    rows = pos_chunk * H_ATTN
    nj = rows_all // rows
    shift = T * H_ATTN
    per_chunk = rows // shift
    zblk = lambda c0: pl.BlockSpec((1, T, W_ATTN), lambda b, j, c0=c0: (b, 0, c0 // H_ATTN))
    cblk = pl.BlockSpec((1, rows, HEAD_DIM), lambda b, j: (b, j, 0))
    halo = pl.BlockSpec((1, shift, HEAD_DIM), lambda b, j: (b, jnp.minimum((j + 1) * per_chunk, nj * per_chunk - 1), 0))
    tab = pl.BlockSpec((T, HEAD_DIM), lambda b, j: (0, 0))
    vec = pl.BlockSpec((1, HEAD_DIM), lambda b, j: (0, 0))
    nbr = len(DIL_PAIRS)
    blocks = 4 * _nbytes((rows, HEAD_DIM), F32) + 2 * _nbytes((shift, HEAD_DIM), F32) + 4 * _nbytes((T, W_ATTN), F32)
    scratch_dims = [(H_ATTN, T, HEAD_DIM), (H_ATTN, T, HEAD_DIM), (nbr, H_ATTN, T, 1), (nbr, H_ATTN, T, 1),
                    (nbr, H_ATTN, T, HEAD_DIM)]
    return pl.pallas_call(
        _attn_sample_kernel,
        grid=(B, nj),
        in_specs=[zblk(COL_AQ), zblk(COL_AK), zblk(COL_AV), cblk, cblk, halo, halo, tab, tab, vec, vec],
        out_specs=[pl.BlockSpec((1, T, W_ATTN), lambda b, j: (b, 0, 0)), cblk, cblk],
        out_shape=[jax.ShapeDtypeStruct((B, T, W_ATTN), BF16),
                   jax.ShapeDtypeStruct((B, rows_all, HEAD_DIM), F32),
                   jax.ShapeDtypeStruct((B, rows_all, HEAD_DIM), F32)],
        scratch_shapes=[pltpu.VMEM(d, F32) for d in scratch_dims],
        compiler_params=pltpu.CompilerParams(
            dimension_semantics=("parallel", "arbitrary"),
            vmem_limit_bytes=_vmem_limit(blocks, sum(_nbytes(d, F32) for d in scratch_dims))),
        name="attn_sample",
    )(z3, z3, z3, cache_k, cache_v, cache_k, cache_v, cos2, sin2, q_norm_g, k_norm_g)


def _route(logits):
    lane = lax.broadcasted_iota(jnp.int32, logits.shape, 1).astype(F32)
    first_max = lambda p, top, ok: jnp.min(jnp.where(ok & (p == top), lane, float(LANES)), -1, keepdims=True)
    is_group = lane < N_GROUPS
    gl = jnp.where(is_group, logits, -jnp.inf)
    pg = jnp.exp(gl - jnp.max(gl, -1, keepdims=True))
    pg = pg / jnp.sum(pg, -1, keepdims=True)
    pg_top = jnp.max(pg, -1, keepdims=True)
    g_top = first_max(pg, pg_top, is_group)
    lo = N_GROUPS + g_top * EXPERTS_PER_GROUP
    in_group = (lane >= lo) & (lane < lo + EXPERTS_PER_GROUP)
    el = jnp.where(in_group, logits, -jnp.inf)
    pe = jnp.exp(el - jnp.max(el, -1, keepdims=True))
    pe = pe / jnp.sum(pe, -1, keepdims=True)
    p1 = jnp.max(pe, -1, keepdims=True)
    l1 = first_max(pe, p1, in_group)
    rest = in_group & (lane != l1)
    p2 = jnp.max(jnp.where(rest, pe, -1.0), -1, keepdims=True)
    l2 = first_max(pe, p2, rest)
    denom = p1 + p2
    gates = jnp.where(lane == 0.0, pg_top * p1 / denom, jnp.where(lane == 1.0, pg_top * p2 / denom, 0.0))
    experts = jnp.where(lane == 0.0, l1 - N_GROUPS, jnp.where(lane == 1.0, l2 - N_GROUPS, 0.0))
    return gates, experts.astype(jnp.int32)


def _out_proj_kernel(oa_ref, ob_ref, x_ref, wa_ref, wb_ref, g_ref, wr_ref, br_ref, h_ref, gate_ref, exp_ref):
    h = (x_ref[...] + jnp.dot(oa_ref[...], wa_ref[...], preferred_element_type=F32)
         + jnp.dot(ob_ref[...], wb_ref[...], preferred_element_type=F32))
    h_ref[...] = h
    logits = jnp.dot(_rms(h, g_ref[...]), wr_ref[...], precision=HIGHEST, preferred_element_type=F32) + br_ref[...]
    gate_ref[...], exp_ref[...] = _route(logits)


def _out_proj(o_a, o_b, x2d, w_a, w_b, g, w_rt, b_rt, tm):
    T = x2d.shape[0]
    row = lambda w: pl.BlockSpec((tm, w), lambda i: (i, 0))
    full = lambda shape: pl.BlockSpec(shape, lambda i: (0, 0))
    blocks = (2 * _nbytes((tm, W_DELTA), BF16) + 2 * _nbytes((tm, D_MODEL), F32) + 2 * _nbytes((W_DELTA, D_MODEL), BF16)
              + _nbytes((D_MODEL, LANES), F32) + 2 * _nbytes((tm, LANES), F32))
    return pl.pallas_call(
        _out_proj_kernel,
        grid=(T // tm,),
        in_specs=[row(W_DELTA), row(W_ATTN), row(D_MODEL), full((W_DELTA, D_MODEL)), full((W_ATTN, D_MODEL)),
                  full((1, D_MODEL)), full((D_MODEL, LANES)), full((1, LANES))],
        out_specs=[row(D_MODEL), row(LANES), row(LANES)],
        out_shape=[jax.ShapeDtypeStruct((T, D_MODEL), F32), jax.ShapeDtypeStruct((T, LANES), F32),
                   jax.ShapeDtypeStruct((T, LANES), jnp.int32)],
        compiler_params=pltpu.CompilerParams(
            dimension_semantics=("parallel",), vmem_limit_bytes=_vmem_limit(blocks)),
        name="out_proj",
    )(o_a, o_b, x2d, w_a, w_b, g, w_rt, b_rt)


def _row_gather(idx_ref, n, src_hbm, dst, sem):
    def body(r, carry):
        pltpu.make_async_copy(src_hbm.at[pl.ds(idx_ref[0, 0, r], 1)], dst.at[pl.ds(r, 1)], sem).start()
        return carry

    lax.fori_loop(0, n, body, 0, unroll=8)


def _expert_kernel(be_ref, nv_ref, tok_ref, tok_next_ref, h_hbm, g_ref, wg_ref, wu_ref, wd_ref, y_ref, xbuf, sem):
    del be_ref
    tm = xbuf.shape[1]
    i = pl.program_id(0)
    n_valid = nv_ref[0]
    slot = lax.rem(i, 2)

    @pl.when((i == 0) & (n_valid > 0))
    def _():
        _row_gather(tok_ref, tm, h_hbm, xbuf.at[0], sem.at[0])

    @pl.when(i + 1 < n_valid)
    def _():
        _row_gather(tok_next_ref, tm, h_hbm, xbuf.at[1 - slot], sem.at[1 - slot])

    @pl.when(i < n_valid)
    def _():
        pltpu.make_async_copy(h_hbm.at[pl.ds(0, tm)], xbuf.at[slot], sem.at[slot]).wait()
        x = _rms(xbuf[slot], g_ref[...]).astype(BF16)
        a = jnp.dot(x, wg_ref[0], preferred_element_type=F32)
        b = jnp.dot(x, wu_ref[0], preferred_element_type=F32)
        hmid = (a * _sigmoid(a)) * b
        y_ref[...] = jnp.dot(hmid.astype(BF16), wd_ref[0], preferred_element_type=F32)

    @pl.when(i >= n_valid)
    def _():
        y_ref[...] = jnp.zeros(y_ref.shape, F32)


def _experts(block_e, n_valid, rows_tok, h, g, w_g, w_u, w_d, tm):
    nb = rows_tok.shape[0]
    last = lambda i, nv: jnp.minimum(i, jnp.maximum(nv[0] - 1, 0))
    smem_blk = lambda f: pl.BlockSpec((1, 1, tm), f, memory_space=pltpu.SMEM)
    blocks = 3 * _nbytes((D_MODEL, EXPERT_FF), BF16) + _nbytes((tm, D_MODEL), F32)
    scratch_bytes = 2 * _nbytes((tm, D_MODEL), F32)
    return pl.pallas_call(
        _expert_kernel,
        grid_spec=pltpu.PrefetchScalarGridSpec(
            num_scalar_prefetch=2,
            grid=(nb,),
            in_specs=[
                smem_blk(lambda i, be, nv: (last(i, nv), 0, 0)),
                smem_blk(lambda i, be, nv: (last(i + 1, nv), 0, 0)),
                pl.BlockSpec(memory_space=pl.ANY),
                pl.BlockSpec((1, D_MODEL), lambda i, be, nv: (0, 0)),
                pl.BlockSpec((1, D_MODEL, EXPERT_FF), lambda i, be, nv: (be[last(i, nv)], 0, 0)),
                pl.BlockSpec((1, D_MODEL, EXPERT_FF), lambda i, be, nv: (be[last(i, nv)], 0, 0)),
                pl.BlockSpec((1, EXPERT_FF, D_MODEL), lambda i, be, nv: (be[last(i, nv)], 0, 0)),
            ],
            out_specs=pl.BlockSpec((tm, D_MODEL), lambda i, be, nv: (i, 0)),
            scratch_shapes=[pltpu.VMEM((2, tm, D_MODEL), F32), pltpu.SemaphoreType.DMA((2,))],
        ),
        out_shape=jax.ShapeDtypeStruct((nb * tm, D_MODEL), F32),
        compiler_params=pltpu.CompilerParams(
            dimension_semantics=("arbitrary",), vmem_limit_bytes=_vmem_limit(blocks, scratch_bytes)),
        name="experts",
    )(block_e, n_valid, rows_tok, rows_tok, h, g, w_g, w_u, w_d)


def _combine_kernel(dst_ref, dst_next_ref, yb_hbm, h_ref, gate_ref, y_ref, ybuf, sem):
    tt = h_ref.shape[0]
    n = TOP_K * tt
    i = pl.program_id(0)
    slot = lax.rem(i, 2)

    @pl.when(i == 0)
    def _():
        _row_gather(dst_ref, n, yb_hbm, ybuf.at[0], sem.at[0])

    @pl.when(i + 1 < pl.num_programs(0))
    def _():
        _row_gather(dst_next_ref, n, yb_hbm, ybuf.at[1 - slot], sem.at[1 - slot])

    pltpu.make_async_copy(yb_hbm.at[pl.ds(0, n)], ybuf.at[slot], sem.at[slot]).wait()
    gates = gate_ref[...]
    moe = gates[:, 0:1] * ybuf[slot, 0:tt, :]
    for k in range(1, TOP_K):
        moe = moe + gates[:, k:k + 1] * ybuf[slot, k * tt:(k + 1) * tt, :]
    y_ref[...] = h_ref[...] + moe


def _combine(dest_blocks, yb, h, gates, tt):
    T = h.shape[0]
    n = TOP_K * tt
    nblk = T // tt
    smem_blk = lambda f: pl.BlockSpec((1, 1, n), f, memory_space=pltpu.SMEM)
    blocks = 2 * _nbytes((tt, D_MODEL), F32) + _nbytes((tt, LANES), F32)
    scratch_bytes = 2 * _nbytes((n, D_MODEL), F32)
    return pl.pallas_call(
        _combine_kernel,
        grid=(nblk,),
        in_specs=[smem_blk(lambda i: (i, 0, 0)), smem_blk(lambda i: (jnp.minimum(i + 1, nblk - 1), 0, 0)),
                  pl.BlockSpec(memory_space=pl.ANY),
                  pl.BlockSpec((tt, D_MODEL), lambda i: (i, 0)), pl.BlockSpec((tt, LANES), lambda i: (i, 0))],
        out_specs=pl.BlockSpec((tt, D_MODEL), lambda i: (i, 0)),
        out_shape=jax.ShapeDtypeStruct((T, D_MODEL), F32),
        scratch_shapes=[pltpu.VMEM((2, n, D_MODEL), F32), pltpu.SemaphoreType.DMA((2,))],
        compiler_params=pltpu.CompilerParams(
            dimension_semantics=("arbitrary",), vmem_limit_bytes=_vmem_limit(blocks, scratch_bytes)),
        name="combine",
    )(dest_blocks, dest_blocks, yb, h, gates)


def _moe(h, gates, experts, g, w_g, w_u, w_d, tm, tt):
    T = h.shape[0]
    A = T * TOP_K
    flat_e = experts[:, :TOP_K].reshape(A)
    flat_tok = jnp.repeat(jnp.arange(T, dtype=jnp.int32), TOP_K)
    order = jnp.argsort(flat_e)
    se, stok = flat_e[order], flat_tok[order]
    counts = jnp.bincount(flat_e, length=N_EXPERTS)
    starts = jnp.cumsum(counts) - counts
    pcounts = (counts + tm - 1) // tm * tm
    pends = jnp.cumsum(pcounts)
    pstarts = pends - pcounts
    dest_sorted = (pstarts[se] + jnp.arange(A) - starts[se]).astype(jnp.int32)
    nb = -(-A // tm) + N_EXPERTS
    rows_tok = jnp.zeros((nb * tm,), jnp.int32).at[dest_sorted].set(stok)
    block_e = jnp.minimum(jnp.searchsorted(pends, jnp.arange(nb) * tm, side='right'), N_EXPERTS - 1).astype(jnp.int32)
    n_valid = (pends[-1:] // tm).astype(jnp.int32)
    yb = _experts(block_e, n_valid, rows_tok.reshape(nb, 1, tm), h, g, w_g, w_u, w_d, tm)
    dest = jnp.zeros((A,), jnp.int32).at[order].set(dest_sorted).reshape(T // tt, tt, TOP_K)
    dest_blocks = dest.transpose(0, 2, 1).reshape(T // tt, 1, TOP_K * tt)
    return _combine(dest_blocks, yb, h, gates, tt)


def _rope_tables(pos):
    half = HEAD_DIM // 2
    inv = ROPE_THETA ** (-jnp.arange(half, dtype=F32) / half)
    ang = pos.astype(F32)[:, None] * inv[None, :]
    cos, sin = jnp.cos(ang), jnp.sin(ang)
    return jnp.concatenate([cos, cos], -1), jnp.concatenate([-sin, sin], -1)


def _layer(x, pos, w, tm, lt, chunk, moe_tm, moe_tt, conv_prev=None, s0=None, win_k=None, win_v=None):
    B, L, D = x.shape
    T = B * L
    x2d = x.reshape(T, D)
    z, zs = _in_proj(x2d, w['norm1_g'], w['w_main'], w['w_small'], tm)
    z3 = z.reshape(B, L, MAIN_COLS)
    zs3 = zs.reshape(B, L, LANES)
    if conv_prev is None:
        o_a, s_new, conv_new = _delta_prompt(z3, zs3, w['conv_w'], w['a_vec'], w['dt_vec'], w['o_norm_g'], lt)
    else:
        o_a, s_new, conv_new = _delta_mixer(z3, zs3, w['conv_w'], w['a_vec'], w['dt_vec'], w['o_norm_g'], lt, chunk,
                                            conv_prev, s0)
    cos2, sin2 = _rope_tables(pos)
    if win_k is None:
        o_b, wk_new, wv_new = _attn_prompt(z3, cos2, sin2, w['q_norm_g'], w['k_norm_g'])
    else:
        rows = win_k.shape[1] * H_ATTN
        o_b, wk_new, wv_new = _attn_sample(z3, win_k.reshape(B, rows, HEAD_DIM), win_v.reshape(B, rows, HEAD_DIM),
                                           cos2, sin2, w['q_norm_g'], w['k_norm_g'])
    h, gates, experts = _out_proj(o_a.reshape(T, W_DELTA), o_b.reshape(T, W_ATTN), x2d, w['w_out_a'], w['w_out_b'],
                                  w['norm2_g'], w['w_rt'], w['b_rt'], tm)
    y = _moe(h, gates, experts, w['norm2_g'], w['w_g'], w['w_u'], w['w_d'], moe_tm, moe_tt)
    keep = wk_new.size // (B * W_ATTN)
    return (y.reshape(B, L, D), conv_new, s_new,
            wk_new.reshape(B, keep, H_ATTN, HEAD_DIM), wv_new.reshape(B, keep, H_ATTN, HEAD_DIM))


def _prep_weights(norm1_g, w_in, conv_w, A_log, dt_bias, o_norm_g, q_norm_g, k_norm_g, w_out, norm2_g,
                  w_group, b_group, w_router, b_router, w_exp_gate, w_exp_up, w_exp_down):
    sizes = [W_DELTA, W_DELTA, W_DELTA, H_DELTA, H_DELTA, W_DELTA, W_ATTN, W_ATTN, W_ATTN]
    zq, zk, zv, zb, za, zg, aq, ak, av = jnp.split(w_in, np.cumsum(sizes)[:-1].tolist(), axis=-1)
    pad = LANES - 2 * H_DELTA
    lane_vec = lambda v: jnp.pad(v.astype(F32), (H_DELTA, LANES - 2 * H_DELTA))[None, :]
    return dict(
        norm1_g=norm1_g[None, :],
        w_main=jnp.concatenate([zq, zk, zv, zg, aq, ak, av], -1).astype(BF16),
        w_small=jnp.pad(jnp.concatenate([zb, za], -1), ((0, 0), (0, pad))).astype(BF16),
        conv_w=conv_w,
        a_vec=lane_vec(A_log), dt_vec=lane_vec(dt_bias),
        o_norm_g=o_norm_g[None, :], q_norm_g=q_norm_g[None, :], k_norm_g=k_norm_g[None, :],
        w_out_a=w_out[:W_DELTA].astype(BF16), w_out_b=w_out[W_DELTA:].astype(BF16),
        norm2_g=norm2_g[None, :],
        w_rt=jnp.pad(jnp.concatenate([w_group, w_router], -1), ((0, 0), (0, LANES - N_GROUPS - N_EXPERTS))),
        b_rt=jnp.pad(jnp.concatenate([b_group, b_router], -1), (0, LANES - N_GROUPS - N_EXPERTS))[None, :],
        w_g=w_exp_gate.astype(BF16), w_u=w_exp_up.astype(BF16), w_d=w_exp_down.astype(BF16),
    )


def kernel(x_prompt, x_sample, state_conv, state_delta, cache_win_k, cache_win_v, norm1_g, w_in, conv_w, A_log, dt_bias, o_norm_g, q_norm_g, k_norm_g, w_out, norm2_g, w_group, b_group, w_router, b_router, w_exp_gate, w_exp_up, w_exp_down):
    depth = w_in.shape[0]
    pos_p = jnp.arange(x_prompt.shape[1], dtype=jnp.int32)
    pos_s = PAST_LEN + jnp.arange(x_sample.shape[1], dtype=jnp.int32)
    yp, ys = x_prompt, x_sample
    outs = [[] for _ in range(8)]
    for l in range(depth):
        w = _prep_weights(norm1_g[l], w_in[l], conv_w[l], A_log[l], dt_bias[l], o_norm_g[l], q_norm_g[l],
                          k_norm_g[l], w_out[l], norm2_g[l], w_group[l], b_group[l], w_router[l], b_router[l],
                          w_exp_gate[l], w_exp_up[l], w_exp_down[l])
        yp, c1, s1, k1, v1 = _layer(yp, pos_p, w, tm=512, lt=256, chunk=CHUNK, moe_tm=512, moe_tt=256)
        ls = x_sample.shape[1]
        ys, c2, s2, k2, v2 = _layer(ys, pos_s, w, tm=x_sample.shape[0] * ls, lt=ls, chunk=min(CHUNK, ls), moe_tm=128,
                                    moe_tt=128,
                                    conv_prev=state_conv[l], s0=state_delta[l],
                                    win_k=cache_win_k[l], win_v=cache_win_v[l])
        for lst, v in zip(outs, (c1, c2, s1, s2, k1, k2, v1, v2)):
            lst.append(v)
    return (yp, ys) + tuple(jnp.stack(o) for o in outs)
```

```python
import functools
import math

import numpy as np
import jax
import jax.numpy as jnp
from jax import lax
from jax.experimental import pallas as pl
from jax.experimental.pallas import tpu as pltpu

F32 = jnp.float32
BF16 = jnp.bfloat16
HIGHEST = lax.Precision.HIGHEST

D_MODEL = 2048
HEAD_DIM = 128
H_DELTA = 8
H_ATTN = 8
W_DELTA = H_DELTA * HEAD_DIM
W_ATTN = H_ATTN * HEAD_DIM
CONV_W = 4
CHUNK = 64
DIL_PAIRS = ((128, 1), (512, 4), (2048, 16))
ATT_BLOCK = 128
ROPE_THETA = 10000.0
N_GROUPS = 4
EXPERTS_PER_GROUP = 8
N_EXPERTS = N_GROUPS * EXPERTS_PER_GROUP
TOP_K = 2
EXPERT_FF = 768
EPS = 1e-6
PAST_LEN = 16384

LANES = 128
SUBLANES = 8
V7X_VMEM_BYTES = 64 * 1024 * 1024
COMPILER_TEMP_BYTES = 12 * 1024 * 1024

COL_ZQ, COL_ZK, COL_ZV, COL_ZG, COL_AQ, COL_AK, COL_AV = (i * H_DELTA for i in range(7))
MAIN_COLS = 7 * W_DELTA
ATTN_SCALE = HEAD_DIM ** -0.5


def _vmem_limit(block_bytes, scratch_bytes=0):
    return int(min(2 * block_bytes + scratch_bytes + COMPILER_TEMP_BYTES, V7X_VMEM_BYTES - 4 * 1024 * 1024))


def _nbytes(shape, dtype):
    return int(np.prod(shape)) * jnp.dtype(dtype).itemsize


def _rms(x, g):
    return x * lax.rsqrt(jnp.mean(x * x, -1, keepdims=True) + EPS) * g


def _sigmoid(x):
    return 1.0 / (1.0 + jnp.exp(-x))


def _dot_nt(a, b, **kw):
    return lax.dot_general(a, b, (((1,), (1,)), ((), ())), **kw)


def _dot_tn(a, b, **kw):
    return lax.dot_general(a, b, (((0,), (0,)), ((), ())), **kw)


def _in_proj_kernel(x_ref, g_ref, w_ref, ws_ref, z_ref, zs_ref, xn_ref):
    @pl.when(pl.program_id(1) == 0)
    def _():
        xn_ref[...] = _rms(x_ref[...], g_ref[...]).astype(BF16)
        zs_ref[...] = jnp.dot(xn_ref[...], ws_ref[...], preferred_element_type=F32)

    z_ref[...] = jnp.dot(xn_ref[...], w_ref[...], preferred_element_type=F32)


def _in_proj(x2d, g, w_main, w_small, tm, tn=1024):
    T = x2d.shape[0]
    blocks = (_nbytes((tm, D_MODEL), F32) + _nbytes((D_MODEL, tn), BF16) + _nbytes((D_MODEL, LANES), BF16)
              + _nbytes((tm, tn), F32) + _nbytes((tm, LANES), F32))
    return pl.pallas_call(
        _in_proj_kernel,
        grid=(T // tm, MAIN_COLS // tn),
        in_specs=[
            pl.BlockSpec((tm, D_MODEL), lambda i, j: (i, 0)),
            pl.BlockSpec((1, D_MODEL), lambda i, j: (0, 0)),
            pl.BlockSpec((D_MODEL, tn), lambda i, j: (0, j)),
            pl.BlockSpec((D_MODEL, LANES), lambda i, j: (0, 0)),
        ],
        out_specs=[
            pl.BlockSpec((tm, tn), lambda i, j: (i, j)),
            pl.BlockSpec((tm, LANES), lambda i, j: (i, 0)),
        ],
        out_shape=[jax.ShapeDtypeStruct((T, MAIN_COLS), F32), jax.ShapeDtypeStruct((T, LANES), F32)],
        scratch_shapes=[pltpu.VMEM((tm, D_MODEL), BF16)],
        compiler_params=pltpu.CompilerParams(
            dimension_semantics=("parallel", "arbitrary"),
            vmem_limit_bytes=_vmem_limit(blocks, _nbytes((tm, D_MODEL), BF16))),
        name="in_proj",
    )(x2d, g, w_main, w_small)


def _delta_chunk(cq, ck, cv, beta_b, g_b, S):
    C = cq.shape[0]
    hp = dict(precision=HIGHEST, preferred_element_type=F32)
    q = cq * lax.rsqrt(jnp.sum(cq * cq, -1, keepdims=True) + EPS) * ATTN_SCALE
    k = ck * lax.rsqrt(jnp.sum(ck * ck, -1, keepdims=True) + EPS)
    row = lax.broadcasted_iota(jnp.int32, (C, C), 0)
    col = lax.broadcasted_iota(jnp.int32, (C, C), 1)
    eye = (row == col).astype(F32)
    gam_b = jnp.dot((row >= col).astype(F32), g_b, **hp)
    gam_col = gam_b[:, :C]
    gam_row = jnp.dot(jnp.ones((C, C), F32), eye * gam_col, **hp)
    dec = jnp.exp(jnp.minimum(gam_col - gam_row, 0.0))
    dec_strict = jnp.where(row > col, dec, 0.0)
    dec_incl = jnp.where(row >= col, dec, 0.0)
    kq = _dot_nt(jnp.concatenate([k, q], 0), k, **hp)
    qk = kq[C:] * dec_incl
    x = -(beta_b[:, :C] * kq[:C] * dec_strict)
    inv = eye + x
    for _ in range(int(math.log2(C)) - 1):
        x = jnp.dot(x, x, **hp)
        inv = inv + jnp.dot(inv, x, **hp)
    eg = jnp.exp(gam_b)
    uw = jnp.dot(inv, jnp.concatenate([beta_b * cv, beta_b * eg * k], 1), **hp)
    u, w = uw[:, :HEAD_DIM], uw[:, HEAD_DIM:]
    gam_last = gam_b[C - 1:C, :]
    kdec = k * jnp.exp(gam_last - gam_b)
    wq_s = _dot_nt(jnp.concatenate([w, q], 0), S, **hp)
    delta = u - wq_s[:C]
    o = eg * wq_s[C:] + jnp.dot(qk, delta, **hp)
    s_new = jnp.exp(gam_last) * S + _dot_tn(delta, kdec, **hp)
    return o, s_new


def _delta_kernel(*refs, lt, chunk, has_state):
    if has_state:
        (zq_ref, zk_ref, zv_ref, zg_ref, zs_ref, cw_ref, a_ref, dt_ref, og_ref, prev_ref, s0_ref,
         o_ref, sout_ref, cout_ref, u_s, c_s, s_s) = refs
    else:
        (zq_ref, zk_ref, zv_ref, zg_ref, zs_ref, cw_ref, a_ref, dt_ref, og_ref,
         o_ref, sout_ref, cout_ref, u_s, c_s, s_s) = refs
    l = pl.program_id(1)
    halo = SUBLANES
    width = 3 * W_DELTA

    @pl.when(l == 0)
    def _():
        if has_state:
            u_s[0:halo, :] = jnp.zeros((halo, width), F32)
            u_s[halo - (CONV_W - 1):halo, :] = prev_ref[0]
            s_s[...] = s0_ref[0]
        else:
            u_s[0:halo, :] = jnp.zeros((halo, width), F32)
            s_s[...] = jnp.zeros(s_s.shape, F32)

    @pl.when(l > 0)
    def _():
        u_s[0:halo, :] = u_s[lt:lt + halo, :]

    u_s[halo:halo + lt, 0:W_DELTA] = zq_ref[0]
    u_s[halo:halo + lt, W_DELTA:2 * W_DELTA] = zk_ref[0]
    u_s[halo:halo + lt, 2 * W_DELTA:width] = zv_ref[0]

    base = halo - (CONV_W - 1)
    acc = cw_ref[0:1, :] * u_s[base:base + lt, :]
    for i in range(1, CONV_W):
        acc = acc + cw_ref[i:i + 1, :] * u_s[base + i:base + i + lt, :]
    c_s[...] = acc * _sigmoid(acc)

    neg_a = -jnp.exp(a_ref[...])

    def chunk_body(ci, carry):
        r0 = pl.multiple_of(ci * chunk, chunk)
        zs = zs_ref[0, pl.ds(r0, chunk), :]
        beta_all = _sigmoid(zs)
        sp_in = zs + dt_ref[...]
        g_all = neg_a * (jnp.maximum(sp_in, 0.0) + jnp.log1p(jnp.exp(-jnp.abs(sp_in))))
        for h in range(H_DELTA):
            lo = h * HEAD_DIM
            cq = c_s[pl.ds(r0, chunk), lo:lo + HEAD_DIM]
            ck = c_s[pl.ds(r0, chunk), W_DELTA + lo:W_DELTA + lo + HEAD_DIM]
            cv = c_s[pl.ds(r0, chunk), 2 * W_DELTA + lo:2 * W_DELTA + lo + HEAD_DIM]
            beta_b = jnp.broadcast_to(beta_all[:, h:h + 1], (chunk, HEAD_DIM))
            g_b = jnp.broadcast_to(g_all[:, H_DELTA + h:H_DELTA + h + 1], (chunk, HEAD_DIM))
            o, s_new = _delta_chunk(cq, ck, cv, beta_b, g_b, s_s[h])
            s_s[h] = s_new
            zg = zg_ref[0, pl.ds(r0, chunk), lo:lo + HEAD_DIM]
            o = _rms(o, og_ref[...]) * (zg * _sigmoid(zg))
            o_ref[0, pl.ds(r0, chunk), lo:lo + HEAD_DIM] = o.astype(o_ref.dtype)
        return carry

    lax.fori_loop(0, lt // chunk, chunk_body, 0)

    @pl.when(l == pl.num_programs(1) - 1)
    def _():
        sout_ref[0] = s_s[...]
        cout_ref[0] = u_s[halo + lt - (CONV_W - 1):halo + lt, :]


def _delta_mixer(z3, zs3, conv_w, a_vec, dt_vec, o_norm_g, lt, chunk, conv_prev=None, s0=None):
    B, L, _ = z3.shape
    has_state = conv_prev is not None
    width = 3 * W_DELTA
    col_blk = lambda c: pl.BlockSpec((1, lt, W_DELTA), lambda b, l, c=c: (b, l, c))
    full = lambda shape: pl.BlockSpec(shape, lambda b, l: (0,) * len(shape))
    in_specs = [col_blk(0), col_blk(1), col_blk(2), col_blk(3),
                pl.BlockSpec((1, lt, LANES), lambda b, l: (b, l, 0)),
                full((CONV_W, width)), full((1, LANES)), full((1, LANES)), full((1, HEAD_DIM))]
    args = [z3, z3, z3, z3, zs3, conv_w, a_vec, dt_vec, o_norm_g]
    if has_state:
        in_specs += [pl.BlockSpec((1, CONV_W - 1, width), lambda b, l: (b, 0, 0)),
                     pl.BlockSpec((1, H_DELTA, HEAD_DIM, HEAD_DIM), lambda b, l: (b, 0, 0, 0))]
        args += [conv_prev, s0]
    blocks = (4 * _nbytes((lt, W_DELTA), F32) + _nbytes((lt, LANES), F32) + _nbytes((CONV_W, width), F32)
              + _nbytes((lt, W_DELTA), BF16) + 2 * _nbytes((H_DELTA, HEAD_DIM, HEAD_DIM), F32)
              + 2 * _nbytes((SUBLANES, width), F32))
    scratch = [pltpu.VMEM((SUBLANES + lt, width), F32), pltpu.VMEM((lt, width), F32),
               pltpu.VMEM((H_DELTA, HEAD_DIM, HEAD_DIM), F32)]
    scratch_bytes = (_nbytes((SUBLANES + lt, width), F32) + _nbytes((lt, width), F32)
                     + _nbytes((H_DELTA, HEAD_DIM, HEAD_DIM), F32))
    return pl.pallas_call(
        functools.partial(_delta_kernel, lt=lt, chunk=chunk, has_state=has_state),
        grid=(B, L // lt),
        in_specs=in_specs,
        out_specs=[
            pl.BlockSpec((1, lt, W_DELTA), lambda b, l: (b, l, 0)),
            pl.BlockSpec((1, H_DELTA, HEAD_DIM, HEAD_DIM), lambda b, l: (b, 0, 0, 0)),
            pl.BlockSpec((1, CONV_W - 1, width), lambda b, l: (b, 0, 0)),
        ],
        out_shape=[jax.ShapeDtypeStruct((B, L, W_DELTA), BF16),
                   jax.ShapeDtypeStruct((B, H_DELTA, HEAD_DIM, HEAD_DIM), F32),
                   jax.ShapeDtypeStruct((B, CONV_W - 1, width), F32)],
        scratch_shapes=scratch,
        compiler_params=pltpu.CompilerParams(
            dimension_semantics=("parallel", "arbitrary"),
            vmem_limit_bytes=_vmem_limit(blocks, scratch_bytes)),
        name="delta_mixer",
    )(*args)


def _split_bf16(x):
    hi = x.astype(BF16)
    return hi, (x - hi.astype(F32)).astype(BF16)


def _dot3(a, b, dot=jnp.dot):
    ah, al = _split_bf16(a)
    bh, bl = _split_bf16(b)
    kw = dict(preferred_element_type=F32)
    return dot(ah, bh, **kw) + dot(ah, bl, **kw) + dot(al, bh, **kw)


def _beta_g(zs, neg_a, dt):
    sp_in = zs + dt
    return _sigmoid(zs), neg_a * (jnp.maximum(sp_in, 0.0) + jnp.log1p(jnp.exp(-jnp.abs(sp_in))))


def _delta_prompt_kernel(zq_ref, zk_ref, zv_ref, zg_ref, zs_ref, cw_ref, a_ref, dt_ref, og_ref,
                         o_ref, sout_ref, cout_ref,
                         u_s, c_s, s_s, uu_s, ww_s, qe_s, kd_s, qk_s, egl_s, *, lt):
    C = CHUNK
    P2 = 2 * C
    n_pairs = H_DELTA // 2
    l = pl.program_id(1)
    halo = SUBLANES
    width = 3 * W_DELTA

    @pl.when(l == 0)
    def _():
        u_s[0:halo, :] = jnp.zeros((halo, width), F32)
        s_s[...] = jnp.zeros(s_s.shape, F32)

    @pl.when(l > 0)
    def _():
        u_s[0:halo, :] = u_s[lt:lt + halo, :]

    u_s[halo:halo + lt, 0:W_DELTA] = zq_ref[0]
    u_s[halo:halo + lt, W_DELTA:2 * W_DELTA] = zk_ref[0]
    u_s[halo:halo + lt, 2 * W_DELTA:width] = zv_ref[0]

    base = halo - (CONV_W - 1)
    acc = cw_ref[0:1, :] * u_s[base:base + lt, :]
    for i in range(1, CONV_W):
        acc = acc + cw_ref[i:i + 1, :] * u_s[base + i:base + i + lt, :]
    c_s[...] = acc * _sigmoid(acc)

    neg_a = -jnp.exp(a_ref[...])
    row = lax.broadcasted_iota(jnp.int32, (P2, P2), 0)
    col = lax.broadcasted_iota(jnp.int32, (P2, P2), 1)
    same_head = (row & C) == (col & C)
    m_strict = jnp.where(same_head & (row > col), 1.0, 0.0)
    m_incl = jnp.where(same_head & (row >= col), 1.0, 0.0)
    eye = jnp.where(row == col, 1.0, 0.0)
    tri = jnp.where(lax.broadcasted_iota(jnp.int32, (C, C), 0) >= lax.broadcasted_iota(jnp.int32, (C, C), 1),
                    1.0, 0.0).astype(BF16)
    lane = lax.broadcasted_iota(jnp.int32, (1, P2), 1)
    head_lanes = lambda h: slice(h * HEAD_DIM, (h + 1) * HEAD_DIM)
    pairs = range(n_pairs)

    def phase_a(ci, carry):
        rows = pl.ds(pl.multiple_of(ci * C, C), C)
        beta_all, g_all = _beta_g(zs_ref[0, rows, :], neg_a, dt_ref[...])
        g1 = g_all.astype(BF16)
        r1 = g_all - g1.astype(F32)
        g2 = r1.astype(BF16)
        g3 = (r1 - g2.astype(F32)).astype(BF16)
        gam_all = (jnp.dot(tri, g1, preferred_element_type=F32) + jnp.dot(tri, g2, preferred_element_type=F32)
                   + jnp.dot(tri, g3, preferred_element_type=F32))
        gam_t = jnp.concatenate([gam_all, gam_all], 0).T

        k2, q2, rhs2, beta2, gcol, grow, qe, kdec = [], [], [], [], [], [], [], []
        for p in pairs:
            ks, qs, vs, bs, gs = [], [], [], [], []
            for h in (2 * p, 2 * p + 1):
                cq = c_s[rows, head_lanes(h)]
                ck = c_s[rows, W_DELTA + h * HEAD_DIM:W_DELTA + (h + 1) * HEAD_DIM]
                cv = c_s[rows, 2 * W_DELTA + h * HEAD_DIM:2 * W_DELTA + (h + 1) * HEAD_DIM]
                q = cq * lax.rsqrt(jnp.sum(cq * cq, -1, keepdims=True) + EPS) * ATTN_SCALE
                k = ck * lax.rsqrt(jnp.sum(ck * ck, -1, keepdims=True) + EPS)
                gam_b = jnp.broadcast_to(gam_all[:, H_DELTA + h:H_DELTA + h + 1], (C, HEAD_DIM))
                gam_last = gam_b[C - 1:C, :]
                kd_s[rows, head_lanes(h)] = k * jnp.exp(gam_last - gam_b)
                egl_s[pl.ds(ci * H_DELTA + h, 1), :] = jnp.exp(gam_last)
                ks.append(k)
                qs.append(q)
                vs.append(cv)
                bs.append(jnp.broadcast_to(beta_all[:, h:h + 1], (C, HEAD_DIM)))
                gs.append(gam_b)
            k2.append(jnp.concatenate(ks, 0))
            q2.append(jnp.concatenate(qs, 0))
            beta2.append(jnp.concatenate(bs, 0))
            gcol.append(jnp.concatenate(gs, 0))
            grow.append(jnp.where(lane < C, gam_t[H_DELTA + 2 * p:H_DELTA + 2 * p + 1, :],
                                  gam_t[H_DELTA + 2 * p + 1:H_DELTA + 2 * p + 2, :]))
            eg = jnp.exp(gcol[p])
            qe.append(eg * q2[p])
            rhs2.append(jnp.concatenate([beta2[p] * jnp.concatenate(vs, 0), beta2[p] * eg * k2[p]], 1))

        kq = [_dot3(jnp.concatenate([k2[p], q2[p]], 0), k2[p], dot=_dot_nt) for p in pairs]
        dec = [jnp.exp(jnp.minimum(gcol[p] - grow[p], 0.0)) for p in pairs]
        for p in pairs:
            qk_s[ci * n_pairs + p] = kq[p][P2:] * (dec[p] * m_incl)
        xs = [-(beta2[p] * kq[p][:P2] * (dec[p] * m_strict)) for p in pairs]
        invs = [eye + x for x in xs]
        xs = [_dot3(x, x) for x in xs]
        for _ in range(int(math.log2(C)) - 2):
            ys = [_dot3(jnp.concatenate([inv, x], 0), x) for inv, x in zip(invs, xs)]
            invs = [inv + y[:P2] for inv, y in zip(invs, ys)]
            xs = [y[P2:] for y in ys]
        invs = [inv + _dot3(inv, x) for inv, x in zip(invs, xs)]
        uw = [_dot3(invs[p], rhs2[p]) for p in pairs]
        for p in pairs:
            for hh in range(2):
                h = 2 * p + hh
                uu_s[rows, head_lanes(h)] = uw[p][hh * C:(hh + 1) * C, :HEAD_DIM]
                ww_s[rows, head_lanes(h)] = uw[p][hh * C:(hh + 1) * C, HEAD_DIM:]
                qe_s[rows, head_lanes(h)] = qe[p][hh * C:(hh + 1) * C, :]
        return carry

    lax.fori_loop(0, lt // C, phase_a, 0)

    def phase_b(ci, carry):
        rows = pl.ds(pl.multiple_of(ci * C, C), C)
        heads = range(H_DELTA)
        wq_s = [_dot3(jnp.concatenate([ww_s[rows, head_lanes(h)], qe_s[rows, head_lanes(h)]], 0), s_s[h], dot=_dot_nt)
                for h in heads]
        delta = [uu_s[rows, head_lanes(h)] - wq_s[h][:C] for h in heads]
        od = [_dot3(qk_s[ci * n_pairs + p], jnp.concatenate([delta[2 * p], delta[2 * p + 1]], 0)) for p in pairs]
        for h in heads:
            o = wq_s[h][C:] + od[h // 2][(h % 2) * C:(h % 2 + 1) * C]
            zg = zg_ref[0, rows, head_lanes(h)]
            o_ref[0, rows, head_lanes(h)] = (_rms(o, og_ref[...]) * (zg * _sigmoid(zg))).astype(o_ref.dtype)
        for h in heads:
            s_s[h] = egl_s[pl.ds(ci * H_DELTA + h, 1), :] * s_s[h] + _dot3(delta[h].T, kd_s[rows, head_lanes(h)])
        return carry

    lax.fori_loop(0, lt // C, phase_b, 0)

    @pl.when(l == pl.num_programs(1) - 1)
    def _():
        sout_ref[0] = s_s[...]
        cout_ref[0] = u_s[halo + lt - (CONV_W - 1):halo + lt, :]


def _delta_prompt(z3, zs3, conv_w, a_vec, dt_vec, o_norm_g, lt):
    B, L, _ = z3.shape
    width = 3 * W_DELTA
    n_chunks = lt // CHUNK
    col_blk = lambda c: pl.BlockSpec((1, lt, W_DELTA), lambda b, l, c=c: (b, l, c))
    full = lambda shape: pl.BlockSpec(shape, lambda b, l: (0,) * len(shape))
    blocks = (4 * _nbytes((lt, W_DELTA), F32) + _nbytes((lt, LANES), F32) + _nbytes((CONV_W, width), F32)
              + _nbytes((lt, W_DELTA), BF16) + _nbytes((H_DELTA, HEAD_DIM, HEAD_DIM), F32)
              + _nbytes((SUBLANES, width), F32))
    scratch_dims = [(SUBLANES + lt, width), (lt, width), (H_DELTA, HEAD_DIM, HEAD_DIM),
                    (lt, W_DELTA), (lt, W_DELTA), (lt, W_DELTA), (lt, W_DELTA),
                    (n_chunks * H_DELTA // 2, 2 * CHUNK, 2 * CHUNK), (n_chunks * H_DELTA, HEAD_DIM)]
    return pl.pallas_call(
        functools.partial(_delta_prompt_kernel, lt=lt),
        grid=(B, L // lt),
        in_specs=[col_blk(0), col_blk(1), col_blk(2), col_blk(3),
                  pl.BlockSpec((1, lt, LANES), lambda b, l: (b, l, 0)),
                  full((CONV_W, width)), full((1, LANES)), full((1, LANES)), full((1, HEAD_DIM))],
        out_specs=[
            pl.BlockSpec((1, lt, W_DELTA), lambda b, l: (b, l, 0)),
            pl.BlockSpec((1, H_DELTA, HEAD_DIM, HEAD_DIM), lambda b, l: (b, 0, 0, 0)),
            pl.BlockSpec((1, CONV_W - 1, width), lambda b, l: (b, 0, 0)),
        ],
        out_shape=[jax.ShapeDtypeStruct((B, L, W_DELTA), BF16),
                   jax.ShapeDtypeStruct((B, H_DELTA, HEAD_DIM, HEAD_DIM), F32),
                   jax.ShapeDtypeStruct((B, CONV_W - 1, width), F32)],
        scratch_shapes=[pltpu.VMEM(d, F32) for d in scratch_dims],
        compiler_params=pltpu.CompilerParams(
            dimension_semantics=("parallel", "arbitrary"),
            vmem_limit_bytes=_vmem_limit(blocks, sum(_nbytes(d, F32) for d in scratch_dims))),
        name="delta_prompt",
    )(z3, z3, z3, z3, zs3, conv_w, a_vec, dt_vec, o_norm_g)


def _norm_rot(x, g, cos2, sin2):
    xn = _rms(x, g)
    return xn * cos2 + pltpu.roll(xn, HEAD_DIM // 2, 1) * sin2


def _softmax_parts(s, mask, vb):
    s = jnp.where(mask, s, -jnp.inf)
    m = jnp.max(s, -1, keepdims=True)
    p = jnp.exp(s - m)
    l = jnp.sum(p, -1, keepdims=True)
    o = jnp.dot(p.astype(BF16), vb.astype(BF16), preferred_element_type=F32)
    return m, l, o


def _merge(parts):
    m_all = functools.reduce(jnp.maximum, [p[0] for p in parts])
    ws = [jnp.exp(p[0] - m_all) for p in parts]
    num = sum(w * p[2] for w, p in zip(ws, parts))
    den = sum(w * p[1] for w, p in zip(ws, parts))
    return num / den


ATT_QSUB = 64


def _band_mask(nq, nk, shift, window):
    dist = (lax.broadcasted_iota(jnp.int32, (nq, nk), 0) + shift) - lax.broadcasted_iota(jnp.int32, (nq, nk), 1)
    return (dist >= 0) & (dist <= window)
def _attn_prompt_kernel(aq_ref, ak_ref, av_ref, cos_ref, sin_ref, qg_ref, kg_ref,
                        ob_ref, kout_ref, vout_ref, q_s, k_s, qd_s, kd_s, vd_s, o_s, m_s, l_s):
    L = q_s.shape[0]
    blk = ATT_BLOCK
    rows = 256

    def prep(i, carry):
        r = pl.multiple_of(i * rows, rows)
        sl = pl.ds(r, rows)
        cos2, sin2 = cos_ref[sl, :], sin_ref[sl, :]
        q = _norm_rot(aq_ref[0, sl, :], qg_ref[...], cos2, sin2) * ATTN_SCALE
        k = _norm_rot(ak_ref[0, sl, :], kg_ref[...], cos2, sin2)
        v = av_ref[0, sl, :]
        q_s[sl, :] = q
        k_s[sl, :] = k
        kout_ref[0, sl, :] = k
        vout_ref[0, sl, :] = v
        qd_s[0, sl, :] = q.astype(BF16)
        kd_s[0, sl, :] = k.astype(BF16)
        vd_s[0, sl, :] = v.astype(BF16)
        return carry

    lax.fori_loop(0, L // rows, prep, 0)

    for branch, (window, dil) in enumerate(DIL_PAIRS):
        assert window // dil == blk
        n = L // dil
        for r in range(dil if dil > 1 else 0):
            src = pl.ds(r, n, stride=dil)
            dst = pl.ds(r * n, n)
            qd_s[branch, dst, :] = q_s[src, :].astype(BF16)
            kd_s[branch, dst, :] = k_s[src, :].astype(BF16)
            vd_s[branch, dst, :] = av_ref.at[0][src, :].astype(BF16)

    qs = ATT_QSUB
    band_mask = _band_mask(qs, qs + blk, blk, blk)
    first_masks = [_band_mask(qs, qo + qs, qo, blk) for qo in range(0, blk, qs)]

    def tiles(dil, r, i, first):
        n = L // dil
        out = []
        for qo in range(0, blk, qs):
            q0 = r * n + i * blk + qo
            k0, nk, mask = (q0 - qo, qo + qs, first_masks[qo // qs]) if first else (q0 - blk, qs + blk, band_mask)
            if not isinstance(q0, int):
                q0, k0 = pl.multiple_of(q0, qs), pl.multiple_of(k0, qs)
            out.append((q0, k0, nk, mask, pl.ds(r + dil * (blk * i + qo), qs, stride=dil if dil > 1 else None)))
        return out

    def run(branch, blocks):
        ts = [t for blk_args in blocks for t in tiles(*blk_args)]
        s = [_dot_nt(qd_s[branch, pl.ds(q0, qs), :], kd_s[branch, pl.ds(k0, nk), :], preferred_element_type=F32)
             for q0, k0, nk, _, _ in ts]
        s = [jnp.where(t[3], x, -jnp.inf) for t, x in zip(ts, s)]
        m = [jnp.max(x, -1, keepdims=True) for x in s]
        p = [jnp.exp(x - mx) for x, mx in zip(s, m)]
        l = [jnp.sum(x, -1, keepdims=True) for x in p]
        o = [jnp.dot(x.astype(BF16), vd_s[branch, pl.ds(t[1], t[2]), :], preferred_element_type=F32)
             for t, x in zip(ts, p)]
        for t, mx, lx, ox in zip(ts, m, l, o):
            o_s.at[branch][t[4], :] = ox
            m_s.at[branch][t[4], :] = jnp.broadcast_to(mx, (qs, HEAD_DIM))
            l_s.at[branch][t[4], :] = jnp.broadcast_to(lx, (qs, HEAD_DIM))

    group = 5
    for branch, (window, dil) in enumerate(DIL_PAIRS):
        nb = L // dil // blk
        if dil == 1:
            assert (nb - 1) % group == 0
            run(branch, [(dil, 0, 0, True)])

            def later(g, carry, branch=branch, dil=dil):
                run(branch, [(dil, 0, 1 + g * group + u, False) for u in range(group)])
                return carry

            lax.fori_loop(0, (nb - 1) // group, later, 0)
        else:
            per_body = max(1, 8 // nb)

            def residues(g, carry, branch=branch, dil=dil, nb=nb, per_body=per_body):
                run(branch, [(dil, g * per_body + u, i, i == 0) for u in range(per_body) for i in range(nb)])
                return carry

            lax.fori_loop(0, dil // per_body, residues, 0)

    def merge(i, carry):
        sl = pl.ds(pl.multiple_of(i * rows, rows), rows)
        parts = [(m_s[b, sl, :], l_s[b, sl, :], o_s[b, sl, :]) for b in range(len(DIL_PAIRS))]
        ob_ref[0, sl, :] = _merge(parts).astype(ob_ref.dtype)
        return carry

    lax.fori_loop(0, L // rows, merge, 0)


def _attn_prompt(z3, cos2, sin2, q_norm_g, k_norm_g):
    B, L, _ = z3.shape
    nbr = len(DIL_PAIRS)
    head_blk = lambda c0: pl.BlockSpec((1, L, HEAD_DIM), lambda b, h, c0=c0: (b, 0, c0 + h))
    out_blk = pl.BlockSpec((1, L, HEAD_DIM), lambda b, h: (b, 0, h))
    tab = pl.BlockSpec((L, HEAD_DIM), lambda b, h: (0, 0))
    vec = pl.BlockSpec((1, HEAD_DIM), lambda b, h: (0, 0))
    blocks = 5 * _nbytes((L, HEAD_DIM), F32) + 2 * _nbytes((L, HEAD_DIM), F32) + _nbytes((L, HEAD_DIM), BF16)
    scratch_bytes = (2 + 3 * nbr) * _nbytes((L, HEAD_DIM), F32) + 3 * nbr * _nbytes((L, HEAD_DIM), BF16)
    return pl.pallas_call(
        _attn_prompt_kernel,
        grid=(B, H_ATTN),
        in_specs=[head_blk(COL_AQ), head_blk(COL_AK), head_blk(COL_AV), tab, tab, vec, vec],
        out_specs=[out_blk, out_blk, out_blk],
        out_shape=[jax.ShapeDtypeStruct((B, L, W_ATTN), BF16),
                   jax.ShapeDtypeStruct((B, L, W_ATTN), F32),
                   jax.ShapeDtypeStruct((B, L, W_ATTN), F32)],
        scratch_shapes=[pltpu.VMEM((L, HEAD_DIM), F32), pltpu.VMEM((L, HEAD_DIM), F32),
                        pltpu.VMEM((nbr, L, HEAD_DIM), BF16), pltpu.VMEM((nbr, L, HEAD_DIM), BF16),
                        pltpu.VMEM((nbr, L, HEAD_DIM), BF16),
                        pltpu.VMEM((nbr, L, HEAD_DIM), F32), pltpu.VMEM((nbr, L, HEAD_DIM), F32),
                        pltpu.VMEM((nbr, L, HEAD_DIM), F32)],
        compiler_params=pltpu.CompilerParams(
            dimension_semantics=("parallel", "parallel"),
            vmem_limit_bytes=_vmem_limit(blocks, scratch_bytes)),
        name="attn_prompt",
    )(z3, z3, z3, cos2, sin2, q_norm_g, k_norm_g)


def _attn_sample_kernel(aq_ref, ak_ref, av_ref, ck_ref, cv_ref, ckh_ref, cvh_ref, cos_ref, sin_ref, qg_ref, kg_ref,
                        ob_ref, wk_ref, wv_ref, q_s, kn_s, m_s, l_s, acc_s):
    T = aq_ref.shape[1]
    H = H_ATTN
    rows = ck_ref.shape[1]
    P = rows // H
    j = pl.program_id(1)
    nj = pl.num_programs(1)
    wbuf = P * nj
    shift = T * H
    nbr = len(DIL_PAIRS)
    head_lanes = lambda h: slice(h * HEAD_DIM, (h + 1) * HEAD_DIM)

    @pl.when(j == 0)
    def _():
        cos2, sin2 = cos_ref[...], sin_ref[...]
        for h in range(H):
            q_s[h] = _norm_rot(aq_ref[0, :, head_lanes(h)], qg_ref[...], cos2, sin2)
            kn_s[h] = _norm_rot(ak_ref[0, :, head_lanes(h)], kg_ref[...], cos2, sin2)
        m_s[...] = jnp.full(m_s.shape, -jnp.inf, F32)
        l_s[...] = jnp.zeros(l_s.shape, F32)
        acc_s[...] = jnp.zeros(acc_s.shape, F32)

    def update(br, h, s, mask, v):
        sm = jnp.where(mask, s, -jnp.inf)
        m_old = m_s[br, h]
        m_new = jnp.maximum(m_old, jnp.max(sm, -1, keepdims=True))
        m_safe = jnp.where(m_new == -jnp.inf, 0.0, m_new)
        p = jnp.exp(sm - m_safe)
        alpha = jnp.exp(m_old - m_safe)
        l_s[br, h] = alpha * l_s[br, h] + jnp.sum(p, -1, keepdims=True)
        acc_s[br, h] = alpha * acc_s[br, h] + jnp.dot(p.astype(BF16), v.astype(BF16), preferred_element_type=F32)
        m_s[br, h] = m_new

    t_c = lax.broadcasted_iota(jnp.int32, (T, P), 0)
    dist_c = wbuf + t_c - (j * P + lax.broadcasted_iota(jnp.int32, (T, P), 1))
    for h in range(H):
        k_h = ck_ref.at[0][pl.ds(h, P, stride=H), :]
        v_h = cv_ref.at[0][pl.ds(h, P, stride=H), :]
        s = _dot_nt(q_s[h].astype(BF16), k_h.astype(BF16), preferred_element_type=F32) * ATTN_SCALE
        for br, (window, dil) in enumerate(DIL_PAIRS):
            update(br, h, s, ((dist_c & (dil - 1)) == 0) & (dist_c <= window), v_h)

    wk_ref[0, 0:rows - shift, :] = ck_ref[0, shift:rows, :]
    wv_ref[0, 0:rows - shift, :] = cv_ref[0, shift:rows, :]

    @pl.when(j < nj - 1)
    def _():
        wk_ref[0, rows - shift:rows, :] = ckh_ref[0]
        wv_ref[0, rows - shift:rows, :] = cvh_ref[0]

    @pl.when(j == nj - 1)
    def _():
        t_n = lax.broadcasted_iota(jnp.int32, (T, T), 0)
        dist_n = t_n - lax.broadcasted_iota(jnp.int32, (T, T), 1)
        for h in range(H):
            k_new = kn_s[h]
            v_new = av_ref[0, :, head_lanes(h)]
            s = _dot_nt(q_s[h].astype(BF16), k_new.astype(BF16), preferred_element_type=F32) * ATTN_SCALE
            for br, (window, dil) in enumerate(DIL_PAIRS):
                update(br, h, s, (dist_n >= 0) & ((dist_n & (dil - 1)) == 0) & (dist_n <= window), v_new)
            parts = [(m_s[br, h], l_s[br, h], acc_s[br, h]) for br in range(nbr)]
            ob_ref[0, :, head_lanes(h)] = _merge(parts).astype(ob_ref.dtype)
            wk_ref.at[0][pl.ds(rows - shift + h, T, stride=H), :] = k_new
            wv_ref.at[0][pl.ds(rows - shift + h, T, stride=H), :] = v_new


def _attn_sample(z3, cache_k, cache_v, cos2, sin2, q_norm_g, k_norm_g, pos_chunk=512):
    B, T, _ = z3.shape
    rows_all = cache_k.shape[1]
    rows = pos_chunk * H_ATTN
    nj = rows_all // rows
    shift = T * H_ATTN
    per_chunk = rows // shift
    zblk = lambda c0: pl.BlockSpec((1, T, W_ATTN), lambda b, j, c0=c0: (b, 0, c0 // H_ATTN))
    cblk = pl.BlockSpec((1, rows, HEAD_DIM), lambda b, j: (b, j, 0))
    halo = pl.BlockSpec((1, shift, HEAD_DIM), lambda b, j: (b, jnp.minimum((j + 1) * per_chunk, nj * per_chunk - 1), 0))
    tab = pl.BlockSpec((T, HEAD_DIM), lambda b, j: (0, 0))
    vec = pl.BlockSpec((1, HEAD_DIM), lambda b, j: (0, 0))
    nbr = len(DIL_PAIRS)
    blocks = 4 * _nbytes((rows, HEAD_DIM), F32) + 2 * _nbytes((shift, HEAD_DIM), F32) + 4 * _nbytes((T, W_ATTN), F32)
    scratch_dims = [(H_ATTN, T, HEAD_DIM), (H_ATTN, T, HEAD_DIM), (nbr, H_ATTN, T, 1), (nbr, H_ATTN, T, 1),
                    (nbr, H_ATTN, T, HEAD_DIM)]
    return pl.pallas_call(
        _attn_sample_kernel,
        grid=(B, nj),
        in_specs=[zblk(COL_AQ), zblk(COL_AK), zblk(COL_AV), cblk, cblk, halo, halo, tab, tab, vec, vec],
        out_specs=[pl.BlockSpec((1, T, W_ATTN), lambda b, j: (b, 0, 0)), cblk, cblk],
        out_shape=[jax.ShapeDtypeStruct((B, T, W_ATTN), BF16),
                   jax.ShapeDtypeStruct((B, rows_all, HEAD_DIM), F32),
                   jax.ShapeDtypeStruct((B, rows_all, HEAD_DIM), F32)],
        scratch_shapes=[pltpu.VMEM(d, F32) for d in scratch_dims],
        compiler_params=pltpu.CompilerParams(
            dimension_semantics=("parallel", "arbitrary"),
            vmem_limit_bytes=_vmem_limit(blocks, sum(_nbytes(d, F32) for d in scratch_dims))),
        name="attn_sample",
    )(z3, z3, z3, cache_k, cache_v, cache_k, cache_v, cos2, sin2, q_norm_g, k_norm_g)


def _route(logits):
    lane = lax.broadcasted_iota(jnp.int32, logits.shape, 1).astype(F32)
    first_max = lambda p, top, ok: jnp.min(jnp.where(ok & (p == top), lane, float(LANES)), -1, keepdims=True)
    is_group = lane < N_GROUPS
    gl = jnp.where(is_group, logits, -jnp.inf)
    pg = jnp.exp(gl - jnp.max(gl, -1, keepdims=True))
    pg = pg / jnp.sum(pg, -1, keepdims=True)
    pg_top = jnp.max(pg, -1, keepdims=True)
    g_top = first_max(pg, pg_top, is_group)
    lo = N_GROUPS + g_top * EXPERTS_PER_GROUP
    in_group = (lane >= lo) & (lane < lo + EXPERTS_PER_GROUP)
    el = jnp.where(in_group, logits, -jnp.inf)
    pe = jnp.exp(el - jnp.max(el, -1, keepdims=True))
    pe = pe / jnp.sum(pe, -1, keepdims=True)
    p1 = jnp.max(pe, -1, keepdims=True)
    l1 = first_max(pe, p1, in_group)
    rest = in_group & (lane != l1)
    p2 = jnp.max(jnp.where(rest, pe, -1.0), -1, keepdims=True)
    l2 = first_max(pe, p2, rest)
    denom = p1 + p2
    gates = jnp.where(lane == 0.0, pg_top * p1 / denom, jnp.where(lane == 1.0, pg_top * p2 / denom, 0.0))
    experts = jnp.where(lane == 0.0, l1 - N_GROUPS, jnp.where(lane == 1.0, l2 - N_GROUPS, 0.0))
    return gates, experts.astype(jnp.int32)


def _out_proj_kernel(oa_ref, ob_ref, x_ref, wa_ref, wb_ref, g_ref, wrh_ref, wrl_ref, br_ref, h_ref, gate_ref, exp_ref):
    h = (x_ref[...] + jnp.dot(oa_ref[...], wa_ref[...], preferred_element_type=F32)
         + jnp.dot(ob_ref[...], wb_ref[...], preferred_element_type=F32))
    h_ref[...] = h
    hn_hi, hn_lo = _split_bf16(_rms(h, g_ref[...]))
    w_hi, w_lo = wrh_ref[...], wrl_ref[...]
    logits = (jnp.dot(hn_hi, w_hi, preferred_element_type=F32) + jnp.dot(hn_hi, w_lo, preferred_element_type=F32)
              + jnp.dot(hn_lo, w_hi, preferred_element_type=F32)) + br_ref[...]
    gate_ref[...], exp_ref[...] = _route(logits)


def _out_proj(o_a, o_b, x2d, w_a, w_b, g, w_rt_hi, w_rt_lo, b_rt, tm):
    T = x2d.shape[0]
    row = lambda w: pl.BlockSpec((tm, w), lambda i: (i, 0))
    full = lambda shape: pl.BlockSpec(shape, lambda i: (0, 0))
    blocks = (2 * _nbytes((tm, W_DELTA), BF16) + 2 * _nbytes((tm, D_MODEL), F32) + 2 * _nbytes((W_DELTA, D_MODEL), BF16)
              + 2 * _nbytes((D_MODEL, LANES), BF16) + 2 * _nbytes((tm, LANES), F32))
    return pl.pallas_call(
        _out_proj_kernel,
        grid=(T // tm,),
        in_specs=[row(W_DELTA), row(W_ATTN), row(D_MODEL), full((W_DELTA, D_MODEL)), full((W_ATTN, D_MODEL)),
                  full((1, D_MODEL)), full((D_MODEL, LANES)), full((D_MODEL, LANES)), full((1, LANES))],
        out_specs=[row(D_MODEL), row(LANES), row(LANES)],
        out_shape=[jax.ShapeDtypeStruct((T, D_MODEL), F32), jax.ShapeDtypeStruct((T, LANES), F32),
                   jax.ShapeDtypeStruct((T, LANES), jnp.int32)],
        compiler_params=pltpu.CompilerParams(
            dimension_semantics=("parallel",), vmem_limit_bytes=_vmem_limit(blocks)),
        name="out_proj",
    )(o_a, o_b, x2d, w_a, w_b, g, w_rt_hi, w_rt_lo, b_rt)


def _row_gather(idx_ref, n, src_hbm, dst, sem):
    def body(r, carry):
        pltpu.make_async_copy(src_hbm.at[pl.ds(idx_ref[0, 0, r], 1)], dst.at[pl.ds(r, 1)], sem).start()
        return carry

    lax.fori_loop(0, n, body, 0, unroll=8)


def _expert_kernel(be_ref, nv_ref, tok_ref, tok_next_ref, h_hbm, g_ref, wg_ref, wu_ref, wd_ref, y_ref, xbuf, sem):
    del be_ref
    tm = xbuf.shape[1]
    i = pl.program_id(0)
    n_valid = nv_ref[0]
    slot = lax.rem(i, 2)

    @pl.when((i == 0) & (n_valid > 0))
    def _():
        _row_gather(tok_ref, tm, h_hbm, xbuf.at[0], sem.at[0])

    @pl.when(i + 1 < n_valid)
    def _():
        _row_gather(tok_next_ref, tm, h_hbm, xbuf.at[1 - slot], sem.at[1 - slot])

    @pl.when(i < n_valid)
    def _():
        pltpu.make_async_copy(h_hbm.at[pl.ds(0, tm)], xbuf.at[slot], sem.at[slot]).wait()
        x = _rms(xbuf[slot], g_ref[...]).astype(BF16)
        a = jnp.dot(x, wg_ref[0], preferred_element_type=F32)
        b = jnp.dot(x, wu_ref[0], preferred_element_type=F32)
        hmid = (a * _sigmoid(a)) * b
        y_ref[...] = jnp.dot(hmid.astype(BF16), wd_ref[0], preferred_element_type=F32)

    @pl.when(i >= n_valid)
    def _():
        y_ref[...] = jnp.zeros(y_ref.shape, F32)


def _experts(block_e, n_valid, rows_tok, h, g, w_g, w_u, w_d, tm):
    nb = rows_tok.shape[0]
    last = lambda i, nv: jnp.minimum(i, jnp.maximum(nv[0] - 1, 0))
    smem_blk = lambda f: pl.BlockSpec((1, 1, tm), f, memory_space=pltpu.SMEM)
    blocks = 3 * _nbytes((D_MODEL, EXPERT_FF), BF16) + _nbytes((tm, D_MODEL), F32)
    scratch_bytes = 2 * _nbytes((tm, D_MODEL), F32)
    return pl.pallas_call(
        _expert_kernel,
        grid_spec=pltpu.PrefetchScalarGridSpec(
            num_scalar_prefetch=2,
            grid=(nb,),
            in_specs=[
                smem_blk(lambda i, be, nv: (last(i, nv), 0, 0)),
                smem_blk(lambda i, be, nv: (last(i + 1, nv), 0, 0)),
                pl.BlockSpec(memory_space=pl.ANY),
                pl.BlockSpec((1, D_MODEL), lambda i, be, nv: (0, 0)),
                pl.BlockSpec((1, D_MODEL, EXPERT_FF), lambda i, be, nv: (be[last(i, nv)], 0, 0)),
                pl.BlockSpec((1, D_MODEL, EXPERT_FF), lambda i, be, nv: (be[last(i, nv)], 0, 0)),
                pl.BlockSpec((1, EXPERT_FF, D_MODEL), lambda i, be, nv: (be[last(i, nv)], 0, 0)),
            ],
            out_specs=pl.BlockSpec((tm, D_MODEL), lambda i, be, nv: (i, 0)),
            scratch_shapes=[pltpu.VMEM((2, tm, D_MODEL), F32), pltpu.SemaphoreType.DMA((2,))],
        ),
        out_shape=jax.ShapeDtypeStruct((nb * tm, D_MODEL), F32),
        compiler_params=pltpu.CompilerParams(
            dimension_semantics=("arbitrary",), vmem_limit_bytes=_vmem_limit(blocks, scratch_bytes)),
        name="experts",
    )(block_e, n_valid, rows_tok, rows_tok, h, g, w_g, w_u, w_d)


def _combine_kernel(dst_ref, dst_next_ref, yb_hbm, h_ref, gate_ref, y_ref, ybuf, sem):
    tt = h_ref.shape[0]
    n = TOP_K * tt
    i = pl.program_id(0)
    slot = lax.rem(i, 2)

    @pl.when(i == 0)
    def _():
        _row_gather(dst_ref, n, yb_hbm, ybuf.at[0], sem.at[0])

    @pl.when(i + 1 < pl.num_programs(0))
    def _():
        _row_gather(dst_next_ref, n, yb_hbm, ybuf.at[1 - slot], sem.at[1 - slot])

    pltpu.make_async_copy(yb_hbm.at[pl.ds(0, n)], ybuf.at[slot], sem.at[slot]).wait()
    gates = gate_ref[...]
    moe = gates[:, 0:1] * ybuf[slot, 0:tt, :]
    for k in range(1, TOP_K):
        moe = moe + gates[:, k:k + 1] * ybuf[slot, k * tt:(k + 1) * tt, :]
    y_ref[...] = h_ref[...] + moe


def _combine(dest_blocks, yb, h, gates, tt):
    T = h.shape[0]
    n = TOP_K * tt
    nblk = T // tt
    smem_blk = lambda f: pl.BlockSpec((1, 1, n), f, memory_space=pltpu.SMEM)
    blocks = 2 * _nbytes((tt, D_MODEL), F32) + _nbytes((tt, LANES), F32)
    scratch_bytes = 2 * _nbytes((n, D_MODEL), F32)
    return pl.pallas_call(
        _combine_kernel,
        grid=(nblk,),
        in_specs=[smem_blk(lambda i: (i, 0, 0)), smem_blk(lambda i: (jnp.minimum(i + 1, nblk - 1), 0, 0)),
                  pl.BlockSpec(memory_space=pl.ANY),
                  pl.BlockSpec((tt, D_MODEL), lambda i: (i, 0)), pl.BlockSpec((tt, LANES), lambda i: (i, 0))],
        out_specs=pl.BlockSpec((tt, D_MODEL), lambda i: (i, 0)),
        out_shape=jax.ShapeDtypeStruct((T, D_MODEL), F32),
        scratch_shapes=[pltpu.VMEM((2, n, D_MODEL), F32), pltpu.SemaphoreType.DMA((2,))],
        compiler_params=pltpu.CompilerParams(
            dimension_semantics=("arbitrary",), vmem_limit_bytes=_vmem_limit(blocks, scratch_bytes)),
        name="combine",
    )(dest_blocks, dest_blocks, yb, h, gates)


def _moe(h, gates, experts, g, w_g, w_u, w_d, tm, tt):
    T = h.shape[0]
    A = T * TOP_K
    flat_e = experts[:, :TOP_K].reshape(A)
    flat_tok = jnp.repeat(jnp.arange(T, dtype=jnp.int32), TOP_K)
    order = jnp.argsort(flat_e)
    se, stok = flat_e[order], flat_tok[order]
    counts = jnp.bincount(flat_e, length=N_EXPERTS)
    starts = jnp.cumsum(counts) - counts
    pcounts = (counts + tm - 1) // tm * tm
    pends = jnp.cumsum(pcounts)
    pstarts = pends - pcounts
    dest_sorted = (pstarts[se] + jnp.arange(A) - starts[se]).astype(jnp.int32)
    nb = -(-A // tm) + N_EXPERTS
    block_e = jnp.minimum(jnp.sum(pends[None, :] <= (jnp.arange(nb) * tm)[:, None], -1), N_EXPERTS - 1).astype(jnp.int32)
    row = jnp.arange(nb * tm, dtype=jnp.int32)
    e_row = block_e[row // tm]
    rank = row - pstarts[e_row]
    rows_tok = jnp.where(rank < counts[e_row], stok[jnp.minimum(starts[e_row] + rank, A - 1)], 0)
    n_valid = (pends[-1:] // tm).astype(jnp.int32)
    yb = _experts(block_e, n_valid, rows_tok.astype(jnp.int32).reshape(nb, 1, tm), h, g, w_g, w_u, w_d, tm)
    dest = dest_sorted[jnp.argsort(order)].reshape(T // tt, tt, TOP_K)
    dest_blocks = dest.transpose(0, 2, 1).reshape(T // tt, 1, TOP_K * tt)
    return _combine(dest_blocks, yb, h, gates, tt)


def _rope_tables(pos):
    half = HEAD_DIM // 2
    inv = ROPE_THETA ** (-jnp.arange(half, dtype=F32) / half)
    ang = pos.astype(F32)[:, None] * inv[None, :]
    cos, sin = jnp.cos(ang), jnp.sin(ang)
    return jnp.concatenate([cos, cos], -1), jnp.concatenate([-sin, sin], -1)


def _layer(x, pos, w, tm, lt, chunk, moe_tm, moe_tt, conv_prev=None, s0=None, win_k=None, win_v=None):
    B, L, D = x.shape
    T = B * L
    x2d = x.reshape(T, D)
    z, zs = _in_proj(x2d, w['norm1_g'], w['w_main'], w['w_small'], min(2 * tm, T))
    z3 = z.reshape(B, L, MAIN_COLS)
    zs3 = zs.reshape(B, L, LANES)
    if conv_prev is None:
        o_a, s_new, conv_new = _delta_prompt(z3, zs3, w['conv_w'], w['a_vec'], w['dt_vec'], w['o_norm_g'], lt)
    else:
        o_a, s_new, conv_new = _delta_mixer(z3, zs3, w['conv_w'], w['a_vec'], w['dt_vec'], w['o_norm_g'], lt, chunk,
                                            conv_prev, s0)
    cos2, sin2 = _rope_tables(pos)
    if win_k is None:
        o_b, wk_new, wv_new = _attn_prompt(z3, cos2, sin2, w['q_norm_g'], w['k_norm_g'])
    else:
        rows = win_k.shape[1] * H_ATTN
        o_b, wk_new, wv_new = _attn_sample(z3, win_k.reshape(B, rows, HEAD_DIM), win_v.reshape(B, rows, HEAD_DIM),
                                           cos2, sin2, w['q_norm_g'], w['k_norm_g'])
    h, gates, experts = _out_proj(o_a.reshape(T, W_DELTA), o_b.reshape(T, W_ATTN), x2d, w['w_out_a'], w['w_out_b'],
                                  w['norm2_g'], w['w_rt_hi'], w['w_rt_lo'], w['b_rt'], tm)
    y = _moe(h, gates, experts, w['norm2_g'], w['w_g'], w['w_u'], w['w_d'], moe_tm, moe_tt)
    keep = wk_new.size // (B * W_ATTN)
    return (y.reshape(B, L, D), conv_new, s_new,
            wk_new.reshape(B, keep, H_ATTN, HEAD_DIM), wv_new.reshape(B, keep, H_ATTN, HEAD_DIM))


def _prep_weights(norm1_g, w_in, conv_w, A_log, dt_bias, o_norm_g, q_norm_g, k_norm_g, w_out, norm2_g,
                  w_group, b_group, w_router, b_router, w_exp_gate, w_exp_up, w_exp_down):
    sizes = [W_DELTA, W_DELTA, W_DELTA, H_DELTA, H_DELTA, W_DELTA, W_ATTN, W_ATTN, W_ATTN]
    zq, zk, zv, zb, za, zg, aq, ak, av = jnp.split(w_in, np.cumsum(sizes)[:-1].tolist(), axis=-1)
    pad = LANES - 2 * H_DELTA
    lane_vec = lambda v: jnp.pad(v.astype(F32), (H_DELTA, LANES - 2 * H_DELTA))[None, :]
    w_rt = jnp.pad(jnp.concatenate([w_group, w_router], -1), ((0, 0), (0, LANES - N_GROUPS - N_EXPERTS)))
    w_rt_hi = w_rt.astype(BF16)
    return dict(
        norm1_g=norm1_g[None, :],
        w_main=jnp.concatenate([zq, zk, zv, zg, aq, ak, av], -1).astype(BF16),
        w_small=jnp.pad(jnp.concatenate([zb, za], -1), ((0, 0), (0, pad))).astype(BF16),
        conv_w=conv_w,
        a_vec=lane_vec(A_log), dt_vec=lane_vec(dt_bias),
        o_norm_g=o_norm_g[None, :], q_norm_g=q_norm_g[None, :], k_norm_g=k_norm_g[None, :],
        w_out_a=w_out[:W_DELTA].astype(BF16), w_out_b=w_out[W_DELTA:].astype(BF16),
        norm2_g=norm2_g[None, :],
        w_rt_hi=w_rt_hi, w_rt_lo=(w_rt - w_rt_hi.astype(F32)).astype(BF16),
        b_rt=jnp.pad(jnp.concatenate([b_group, b_router], -1), (0, LANES - N_GROUPS - N_EXPERTS))[None, :],
        w_g=w_exp_gate.astype(BF16), w_u=w_exp_up.astype(BF16), w_d=w_exp_down.astype(BF16),
    )


def kernel(x_prompt, x_sample, state_conv, state_delta, cache_win_k, cache_win_v, norm1_g, w_in, conv_w, A_log, dt_bias, o_norm_g, q_norm_g, k_norm_g, w_out, norm2_g, w_group, b_group, w_router, b_router, w_exp_gate, w_exp_up, w_exp_down):
    depth = w_in.shape[0]
    pos_p = jnp.arange(x_prompt.shape[1], dtype=jnp.int32)
    pos_s = PAST_LEN + jnp.arange(x_sample.shape[1], dtype=jnp.int32)
    yp, ys = x_prompt, x_sample
    outs = [[] for _ in range(8)]
    for l in range(depth):
        w = _prep_weights(norm1_g[l], w_in[l], conv_w[l], A_log[l], dt_bias[l], o_norm_g[l], q_norm_g[l],
                          k_norm_g[l], w_out[l], norm2_g[l], w_group[l], b_group[l], w_router[l], b_router[l],
                          w_exp_gate[l], w_exp_up[l], w_exp_down[l])
        yp, c1, s1, k1, v1 = _layer(yp, pos_p, w, tm=512, lt=256, chunk=CHUNK, moe_tm=512, moe_tt=256)
        ls = x_sample.shape[1]
        ys, c2, s2, k2, v2 = _layer(ys, pos_s, w, tm=x_sample.shape[0] * ls, lt=ls, chunk=min(CHUNK, ls), moe_tm=128,
                                    moe_tt=128,
                                    conv_prev=state_conv[l], s0=state_delta[l],
                                    win_k=cache_win_k[l], win_v=cache_win_v[l])
        for lst, v in zip(outs, (c1, c2, s1, s2, k1, k2, v1, v2)):
            lst.append(v)
    return (yp, ys) + tuple(jnp.stack(o) for o in outs)
```

```python
import functools
import math

import numpy as np
import jax
import jax.numpy as jnp
from jax import lax
from jax.experimental import pallas as pl
from jax.experimental.pallas import tpu as pltpu

F32 = jnp.float32
BF16 = jnp.bfloat16
HIGHEST = lax.Precision.HIGHEST

D_MODEL = 2048
HEAD_DIM = 128
H_DELTA = 8
H_ATTN = 8
W_DELTA = H_DELTA * HEAD_DIM
W_ATTN = H_ATTN * HEAD_DIM
CONV_W = 4
CHUNK = 64
DIL_PAIRS = ((128, 1), (512, 4), (2048, 16))
ATT_BLOCK = 128
ROPE_THETA = 10000.0
N_GROUPS = 4
EXPERTS_PER_GROUP = 8
N_EXPERTS = N_GROUPS * EXPERTS_PER_GROUP
TOP_K = 2
EXPERT_FF = 768
EPS = 1e-6
PAST_LEN = 16384

LANES = 128
SUBLANES = 8
V7X_VMEM_BYTES = 64 * 1024 * 1024
COMPILER_TEMP_BYTES = 12 * 1024 * 1024

COL_ZQ, COL_ZK, COL_ZV, COL_ZG, COL_AQ, COL_AK, COL_AV = (i * H_DELTA for i in range(7))
MAIN_COLS = 7 * W_DELTA
ATTN_SCALE = HEAD_DIM ** -0.5


def _vmem_limit(block_bytes, scratch_bytes=0):
    return int(min(2 * block_bytes + scratch_bytes + COMPILER_TEMP_BYTES, V7X_VMEM_BYTES - 4 * 1024 * 1024))


def _nbytes(shape, dtype):
    return int(np.prod(shape)) * jnp.dtype(dtype).itemsize


def _rms(x, g):
    return x * lax.rsqrt(jnp.mean(x * x, -1, keepdims=True) + EPS) * g


def _sigmoid(x):
    return 1.0 / (1.0 + jnp.exp(-x))


def _dot_nt(a, b, **kw):
    return lax.dot_general(a, b, (((1,), (1,)), ((), ())), **kw)


def _dot_tn(a, b, **kw):
    return lax.dot_general(a, b, (((0,), (0,)), ((), ())), **kw)


def _in_proj_kernel(x_ref, g_ref, w_ref, ws_ref, z_ref, zs_ref, xn_ref):
    @pl.when(pl.program_id(1) == 0)
    def _():
        xn_ref[...] = _rms(x_ref[...], g_ref[...]).astype(BF16)
        zs_ref[...] = jnp.dot(xn_ref[...], ws_ref[...], preferred_element_type=F32)

    z_ref[...] = jnp.dot(xn_ref[...], w_ref[...], preferred_element_type=F32)


def _in_proj(x2d, g, w_main, w_small, tm, tn=1024):
    T = x2d.shape[0]
    blocks = (_nbytes((tm, D_MODEL), F32) + _nbytes((D_MODEL, tn), BF16) + _nbytes((D_MODEL, LANES), BF16)
              + _nbytes((tm, tn), F32) + _nbytes((tm, LANES), F32))
    return pl.pallas_call(
        _in_proj_kernel,
        grid=(T // tm, MAIN_COLS // tn),
        in_specs=[
            pl.BlockSpec((tm, D_MODEL), lambda i, j: (i, 0)),
            pl.BlockSpec((1, D_MODEL), lambda i, j: (0, 0)),
            pl.BlockSpec((D_MODEL, tn), lambda i, j: (0, j)),
            pl.BlockSpec((D_MODEL, LANES), lambda i, j: (0, 0)),
        ],
        out_specs=[
            pl.BlockSpec((tm, tn), lambda i, j: (i, j)),
            pl.BlockSpec((tm, LANES), lambda i, j: (i, 0)),
        ],
        out_shape=[jax.ShapeDtypeStruct((T, MAIN_COLS), F32), jax.ShapeDtypeStruct((T, LANES), F32)],
        scratch_shapes=[pltpu.VMEM((tm, D_MODEL), BF16)],
        compiler_params=pltpu.CompilerParams(
            dimension_semantics=("parallel", "arbitrary"),
            vmem_limit_bytes=_vmem_limit(blocks, _nbytes((tm, D_MODEL), BF16))),
        name="in_proj",
    )(x2d, g, w_main, w_small)


def _delta_chunk(cq, ck, cv, beta_b, g_b, S):
    C = cq.shape[0]
    q = cq * lax.rsqrt(jnp.sum(cq * cq, -1, keepdims=True) + EPS) * ATTN_SCALE
    k = ck * lax.rsqrt(jnp.sum(ck * ck, -1, keepdims=True) + EPS)
    row = lax.broadcasted_iota(jnp.int32, (C, C), 0)
    col = lax.broadcasted_iota(jnp.int32, (C, C), 1)
    eye = jnp.where(row == col, 1.0, 0.0)
    gam_b = _dot_sel(jnp.where(row >= col, 1.0, 0.0).astype(BF16), g_b)
    gam_col = gam_b[:, :C]
    gam_row = _dot_sel(jnp.ones((C, C), BF16), eye * gam_col)
    dec = jnp.exp(jnp.minimum(gam_col - gam_row, 0.0))
    dec_strict = jnp.where(row > col, dec, 0.0)
    dec_incl = jnp.where(row >= col, dec, 0.0)
    kq = _dot3(jnp.concatenate([k, q], 0), k, dot=_dot_nt)
    qk = kq[C:] * dec_incl
    x = -(beta_b[:, :C] * kq[:C] * dec_strict)
    inv = eye + x
    for _ in range(int(math.log2(C)) - 1):
        x = _dot3(x, x)
        inv = inv + _dot3(inv, x)
    eg = jnp.exp(gam_b)
    uw = _dot3(inv, jnp.concatenate([beta_b * cv, beta_b * eg * k], 1))
    u, w = uw[:, :HEAD_DIM], uw[:, HEAD_DIM:]
    gam_last = gam_b[C - 1:C, :]
    kdec = k * jnp.exp(gam_last - gam_b)
    wq_s = _dot3(jnp.concatenate([w, q], 0), S, dot=_dot_nt)
    delta = u - wq_s[:C]
    o = eg * wq_s[C:] + _dot3(qk, delta)
    s_new = jnp.exp(gam_last) * S + _dot_tn(delta, kdec, precision=HIGHEST, preferred_element_type=F32)
    return o, s_new


def _delta_kernel(*refs, lt, chunk, has_state):
    if has_state:
        (zq_ref, zk_ref, zv_ref, zg_ref, zs_ref, cw_ref, a_ref, dt_ref, og_ref, prev_ref, s0_ref,
         o_ref, sout_ref, cout_ref, u_s, c_s, s_s) = refs
    else:
        (zq_ref, zk_ref, zv_ref, zg_ref, zs_ref, cw_ref, a_ref, dt_ref, og_ref,
         o_ref, sout_ref, cout_ref, u_s, c_s, s_s) = refs
    l = pl.program_id(1)
    halo = SUBLANES
    width = 3 * W_DELTA

    @pl.when(l == 0)
    def _():
        if has_state:
            u_s[0:halo, :] = jnp.zeros((halo, width), F32)
            u_s[halo - (CONV_W - 1):halo, :] = prev_ref[0]
            s_s[...] = s0_ref[0]
        else:
            u_s[0:halo, :] = jnp.zeros((halo, width), F32)
            s_s[...] = jnp.zeros(s_s.shape, F32)

    @pl.when(l > 0)
    def _():
        u_s[0:halo, :] = u_s[lt:lt + halo, :]

    u_s[halo:halo + lt, 0:W_DELTA] = zq_ref[0]
    u_s[halo:halo + lt, W_DELTA:2 * W_DELTA] = zk_ref[0]
    u_s[halo:halo + lt, 2 * W_DELTA:width] = zv_ref[0]

    base = halo - (CONV_W - 1)
    acc = cw_ref[0:1, :] * u_s[base:base + lt, :]
    for i in range(1, CONV_W):
        acc = acc + cw_ref[i:i + 1, :] * u_s[base + i:base + i + lt, :]
    c_s[...] = acc * _sigmoid(acc)

    neg_a = -jnp.exp(a_ref[...])

    def chunk_body(ci, carry):
        r0 = pl.multiple_of(ci * chunk, chunk)
        zs = zs_ref[0, pl.ds(r0, chunk), :]
        beta_all = _sigmoid(zs)
        sp_in = zs + dt_ref[...]
        g_all = neg_a * (jnp.maximum(sp_in, 0.0) + jnp.log1p(jnp.exp(-jnp.abs(sp_in))))
        for h in range(H_DELTA):
            lo = h * HEAD_DIM
            cq = c_s[pl.ds(r0, chunk), lo:lo + HEAD_DIM]
            ck = c_s[pl.ds(r0, chunk), W_DELTA + lo:W_DELTA + lo + HEAD_DIM]
            cv = c_s[pl.ds(r0, chunk), 2 * W_DELTA + lo:2 * W_DELTA + lo + HEAD_DIM]
            beta_b = jnp.broadcast_to(beta_all[:, h:h + 1], (chunk, HEAD_DIM))
            g_b = jnp.broadcast_to(g_all[:, H_DELTA + h:H_DELTA + h + 1], (chunk, HEAD_DIM))
            o, s_new = _delta_chunk(cq, ck, cv, beta_b, g_b, s_s[h])
            s_s[h] = s_new
            zg = zg_ref[0, pl.ds(r0, chunk), lo:lo + HEAD_DIM]
            o = _rms(o, og_ref[...]) * (zg * _sigmoid(zg))
            o_ref[0, pl.ds(r0, chunk), lo:lo + HEAD_DIM] = o.astype(o_ref.dtype)
        return carry

    lax.fori_loop(0, lt // chunk, chunk_body, 0)

    @pl.when(l == pl.num_programs(1) - 1)
    def _():
        sout_ref[0] = s_s[...]
        cout_ref[0] = u_s[halo + lt - (CONV_W - 1):halo + lt, :]


def _delta_mixer(z3, zs3, conv_w, a_vec, dt_vec, o_norm_g, lt, chunk, conv_prev=None, s0=None):
    B, L, _ = z3.shape
    has_state = conv_prev is not None
    width = 3 * W_DELTA
    col_blk = lambda c: pl.BlockSpec((1, lt, W_DELTA), lambda b, l, c=c: (b, l, c))
    full = lambda shape: pl.BlockSpec(shape, lambda b, l: (0,) * len(shape))
    in_specs = [col_blk(0), col_blk(1), col_blk(2), col_blk(3),
                pl.BlockSpec((1, lt, LANES), lambda b, l: (b, l, 0)),
                full((CONV_W, width)), full((1, LANES)), full((1, LANES)), full((1, HEAD_DIM))]
    args = [z3, z3, z3, z3, zs3, conv_w, a_vec, dt_vec, o_norm_g]
    if has_state:
        in_specs += [pl.BlockSpec((1, CONV_W - 1, width), lambda b, l: (b, 0, 0)),
                     pl.BlockSpec((1, H_DELTA, HEAD_DIM, HEAD_DIM), lambda b, l: (b, 0, 0, 0))]
        args += [conv_prev, s0]
    blocks = (4 * _nbytes((lt, W_DELTA), F32) + _nbytes((lt, LANES), F32) + _nbytes((CONV_W, width), F32)
              + _nbytes((lt, W_DELTA), BF16) + 2 * _nbytes((H_DELTA, HEAD_DIM, HEAD_DIM), F32)
              + 2 * _nbytes((SUBLANES, width), F32))
    scratch = [pltpu.VMEM((SUBLANES + lt, width), F32), pltpu.VMEM((lt, width), F32),
               pltpu.VMEM((H_DELTA, HEAD_DIM, HEAD_DIM), F32)]
    scratch_bytes = (_nbytes((SUBLANES + lt, width), F32) + _nbytes((lt, width), F32)
                     + _nbytes((H_DELTA, HEAD_DIM, HEAD_DIM), F32))
    return pl.pallas_call(
        functools.partial(_delta_kernel, lt=lt, chunk=chunk, has_state=has_state),
        grid=(B, L // lt),
        in_specs=in_specs,
        out_specs=[
            pl.BlockSpec((1, lt, W_DELTA), lambda b, l: (b, l, 0)),
            pl.BlockSpec((1, H_DELTA, HEAD_DIM, HEAD_DIM), lambda b, l: (b, 0, 0, 0)),
            pl.BlockSpec((1, CONV_W - 1, width), lambda b, l: (b, 0, 0)),
        ],
        out_shape=[jax.ShapeDtypeStruct((B, L, W_DELTA), BF16),
                   jax.ShapeDtypeStruct((B, H_DELTA, HEAD_DIM, HEAD_DIM), F32),
                   jax.ShapeDtypeStruct((B, CONV_W - 1, width), F32)],
        scratch_shapes=scratch,
        compiler_params=pltpu.CompilerParams(
            dimension_semantics=("parallel", "arbitrary"),
            vmem_limit_bytes=_vmem_limit(blocks, scratch_bytes)),
        name="delta_mixer",
    )(*args)


def _split_bf16(x):
    hi = x.astype(BF16)
    return hi, (x - hi.astype(F32)).astype(BF16)


def _dot3(a, b, dot=jnp.dot):
    ah, al = _split_bf16(a)
    bh, bl = _split_bf16(b)
    kw = dict(preferred_element_type=F32)
    return dot(ah, bh, **kw) + dot(ah, bl, **kw) + dot(al, bh, **kw)


def _dot_sel(sel, x):
    x1 = x.astype(BF16)
    r1 = x - x1.astype(F32)
    x2 = r1.astype(BF16)
    x3 = (r1 - x2.astype(F32)).astype(BF16)
    kw = dict(preferred_element_type=F32)
    return jnp.dot(sel, x1, **kw) + jnp.dot(sel, x2, **kw) + jnp.dot(sel, x3, **kw)


def _beta_g(zs, neg_a, dt):
    sp_in = zs + dt
    return _sigmoid(zs), neg_a * (jnp.maximum(sp_in, 0.0) + jnp.log1p(jnp.exp(-jnp.abs(sp_in))))


def _delta_prompt_kernel(zq_ref, zk_ref, zv_ref, zg_ref, zs_ref, cw_ref, a_ref, dt_ref, og_ref,
                         o_ref, sout_ref, cout_ref,
                         u_s, c_s, s_s, uu_s, ww_s, qe_s, kd_s, qk_s, egl_s, *, lt):
    C = CHUNK
    P2 = 2 * C
    n_pairs = H_DELTA // 2
    l = pl.program_id(1)
    halo = SUBLANES
    width = 3 * W_DELTA

    @pl.when(l == 0)
    def _():
        u_s[0:halo, :] = jnp.zeros((halo, width), F32)
        s_s[...] = jnp.zeros(s_s.shape, F32)

    @pl.when(l > 0)
    def _():
        u_s[0:halo, :] = u_s[lt:lt + halo, :]

    u_s[halo:halo + lt, 0:W_DELTA] = zq_ref[0]
    u_s[halo:halo + lt, W_DELTA:2 * W_DELTA] = zk_ref[0]
    u_s[halo:halo + lt, 2 * W_DELTA:width] = zv_ref[0]

    base = halo - (CONV_W - 1)
    acc = cw_ref[0:1, :] * u_s[base:base + lt, :]
    for i in range(1, CONV_W):
        acc = acc + cw_ref[i:i + 1, :] * u_s[base + i:base + i + lt, :]
    c_s[...] = acc * _sigmoid(acc)

    neg_a = -jnp.exp(a_ref[...])
    row = lax.broadcasted_iota(jnp.int32, (P2, P2), 0)
    col = lax.broadcasted_iota(jnp.int32, (P2, P2), 1)
    same_head = (row & C) == (col & C)
    m_strict = jnp.where(same_head & (row > col), 1.0, 0.0)
    m_incl = jnp.where(same_head & (row >= col), 1.0, 0.0)
    eye = jnp.where(row == col, 1.0, 0.0)
    tri = jnp.where(lax.broadcasted_iota(jnp.int32, (C, C), 0) >= lax.broadcasted_iota(jnp.int32, (C, C), 1),
                    1.0, 0.0).astype(BF16)
    lane = lax.broadcasted_iota(jnp.int32, (1, P2), 1)
    head_lanes = lambda h: slice(h * HEAD_DIM, (h + 1) * HEAD_DIM)
    pairs = range(n_pairs)

    def phase_a(ci, carry):
        rows = pl.ds(pl.multiple_of(ci * C, C), C)
        beta_all, g_all = _beta_g(zs_ref[0, rows, :], neg_a, dt_ref[...])
        gam_all = _dot_sel(tri, g_all)
        gam_t = jnp.concatenate([gam_all, gam_all], 0).T

        k2, q2, rhs2, beta2, gcol, grow, qe, kdec = [], [], [], [], [], [], [], []
        for p in pairs:
            ks, qs, vs, bs, gs = [], [], [], [], []
            for h in (2 * p, 2 * p + 1):
                cq = c_s[rows, head_lanes(h)]
                ck = c_s[rows, W_DELTA + h * HEAD_DIM:W_DELTA + (h + 1) * HEAD_DIM]
                cv = c_s[rows, 2 * W_DELTA + h * HEAD_DIM:2 * W_DELTA + (h + 1) * HEAD_DIM]
                q = cq * lax.rsqrt(jnp.sum(cq * cq, -1, keepdims=True) + EPS) * ATTN_SCALE
                k = ck * lax.rsqrt(jnp.sum(ck * ck, -1, keepdims=True) + EPS)
                gam_b = jnp.broadcast_to(gam_all[:, H_DELTA + h:H_DELTA + h + 1], (C, HEAD_DIM))
                gam_last = gam_b[C - 1:C, :]
                kd_s[rows, head_lanes(h)] = k * jnp.exp(gam_last - gam_b)
                egl_s[pl.ds(ci * H_DELTA + h, 1), :] = jnp.exp(gam_last)
                ks.append(k)
                qs.append(q)
                vs.append(cv)
                bs.append(jnp.broadcast_to(beta_all[:, h:h + 1], (C, HEAD_DIM)))
                gs.append(gam_b)
            k2.append(jnp.concatenate(ks, 0))
            q2.append(jnp.concatenate(qs, 0))
            beta2.append(jnp.concatenate(bs, 0))
            gcol.append(jnp.concatenate(gs, 0))
            grow.append(jnp.where(lane < C, gam_t[H_DELTA + 2 * p:H_DELTA + 2 * p + 1, :],
                                  gam_t[H_DELTA + 2 * p + 1:H_DELTA + 2 * p + 2, :]))
            eg = jnp.exp(gcol[p])
            qe.append(eg * q2[p])
            rhs2.append(jnp.concatenate([beta2[p] * jnp.concatenate(vs, 0), beta2[p] * eg * k2[p]], 1))

        kq = [_dot3(jnp.concatenate([k2[p], q2[p]], 0), k2[p], dot=_dot_nt) for p in pairs]
        dec = [jnp.exp(jnp.minimum(gcol[p] - grow[p], 0.0)) for p in pairs]
        for p in pairs:
            qk_s[ci * n_pairs + p] = kq[p][P2:] * (dec[p] * m_incl)
        xs = [-(beta2[p] * kq[p][:P2] * (dec[p] * m_strict)) for p in pairs]
        invs = [eye + x for x in xs]
        xs = [_dot3(x, x) for x in xs]
        for _ in range(int(math.log2(C)) - 2):
            ys = [_dot3(jnp.concatenate([inv, x], 0), x) for inv, x in zip(invs, xs)]
            invs = [inv + y[:P2] for inv, y in zip(invs, ys)]
            xs = [y[P2:] for y in ys]
        invs = [inv + _dot3(inv, x) for inv, x in zip(invs, xs)]
        uw = [_dot3(invs[p], rhs2[p]) for p in pairs]
        for p in pairs:
            for hh in range(2):
                h = 2 * p + hh
                uu_s[rows, head_lanes(h)] = uw[p][hh * C:(hh + 1) * C, :HEAD_DIM]
                ww_s[rows, head_lanes(h)] = uw[p][hh * C:(hh + 1) * C, HEAD_DIM:]
                qe_s[rows, head_lanes(h)] = qe[p][hh * C:(hh + 1) * C, :]
        return carry

    lax.fori_loop(0, lt // C, phase_a, 0)

    def phase_b(ci, carry):
        rows = pl.ds(pl.multiple_of(ci * C, C), C)
        heads = range(H_DELTA)
        wq_s = [_dot3(jnp.concatenate([ww_s[rows, head_lanes(h)], qe_s[rows, head_lanes(h)]], 0), s_s[h], dot=_dot_nt)
                for h in heads]
        delta = [uu_s[rows, head_lanes(h)] - wq_s[h][:C] for h in heads]
        od = [_dot3(qk_s[ci * n_pairs + p], jnp.concatenate([delta[2 * p], delta[2 * p + 1]], 0)) for p in pairs]
        for h in heads:
            o = wq_s[h][C:] + od[h // 2][(h % 2) * C:(h % 2 + 1) * C]
            zg = zg_ref[0, rows, head_lanes(h)]
            o_ref[0, rows, head_lanes(h)] = (_rms(o, og_ref[...]) * (zg * _sigmoid(zg))).astype(o_ref.dtype)
        for h in heads:
            s_s[h] = egl_s[pl.ds(ci * H_DELTA + h, 1), :] * s_s[h] + _dot3(delta[h].T, kd_s[rows, head_lanes(h)])
        return carry

    lax.fori_loop(0, lt // C, phase_b, 0)

    @pl.when(l == pl.num_programs(1) - 1)
    def _():
        sout_ref[0] = s_s[...]
        cout_ref[0] = u_s[halo + lt - (CONV_W - 1):halo + lt, :]


def _delta_prompt(z3, zs3, conv_w, a_vec, dt_vec, o_norm_g, lt):
    B, L, _ = z3.shape
    width = 3 * W_DELTA
    n_chunks = lt // CHUNK
    col_blk = lambda c: pl.BlockSpec((1, lt, W_DELTA), lambda b, l, c=c: (b, l, c))
    full = lambda shape: pl.BlockSpec(shape, lambda b, l: (0,) * len(shape))
    blocks = (4 * _nbytes((lt, W_DELTA), F32) + _nbytes((lt, LANES), F32) + _nbytes((CONV_W, width), F32)
              + _nbytes((lt, W_DELTA), BF16) + _nbytes((H_DELTA, HEAD_DIM, HEAD_DIM), F32)
              + _nbytes((SUBLANES, width), F32))
    scratch_dims = [(SUBLANES + lt, width), (lt, width), (H_DELTA, HEAD_DIM, HEAD_DIM),
                    (lt, W_DELTA), (lt, W_DELTA), (lt, W_DELTA), (lt, W_DELTA),
                    (n_chunks * H_DELTA // 2, 2 * CHUNK, 2 * CHUNK), (n_chunks * H_DELTA, HEAD_DIM)]
    return pl.pallas_call(
        functools.partial(_delta_prompt_kernel, lt=lt),
        grid=(B, L // lt),
        in_specs=[col_blk(0), col_blk(1), col_blk(2), col_blk(3),
                  pl.BlockSpec((1, lt, LANES), lambda b, l: (b, l, 0)),
                  full((CONV_W, width)), full((1, LANES)), full((1, LANES)), full((1, HEAD_DIM))],
        out_specs=[
            pl.BlockSpec((1, lt, W_DELTA), lambda b, l: (b, l, 0)),
            pl.BlockSpec((1, H_DELTA, HEAD_DIM, HEAD_DIM), lambda b, l: (b, 0, 0, 0)),
            pl.BlockSpec((1, CONV_W - 1, width), lambda b, l: (b, 0, 0)),
        ],
        out_shape=[jax.ShapeDtypeStruct((B, L, W_DELTA), BF16),
                   jax.ShapeDtypeStruct((B, H_DELTA, HEAD_DIM, HEAD_DIM), F32),
                   jax.ShapeDtypeStruct((B, CONV_W - 1, width), F32)],
        scratch_shapes=[pltpu.VMEM(d, F32) for d in scratch_dims],
        compiler_params=pltpu.CompilerParams(
            dimension_semantics=("parallel", "arbitrary"),
            vmem_limit_bytes=_vmem_limit(blocks, sum(_nbytes(d, F32) for d in scratch_dims))),
        name="delta_prompt",
    )(z3, z3, z3, z3, zs3, conv_w, a_vec, dt_vec, o_norm_g)


def _norm_rot(x, g, cos2, sin2):
    xn = _rms(x, g)
    return xn * cos2 + pltpu.roll(xn, HEAD_DIM // 2, 1) * sin2


def _softmax_parts(s, mask, vb):
    s = jnp.where(mask, s, -jnp.inf)
    m = jnp.max(s, -1, keepdims=True)
    p = jnp.exp(s - m)
    l = jnp.sum(p, -1, keepdims=True)
    o = jnp.dot(p.astype(BF16), vb.astype(BF16), preferred_element_type=F32)
    return m, l, o


def _merge(parts):
    m_all = functools.reduce(jnp.maximum, [p[0] for p in parts])
    ws = [jnp.exp(p[0] - m_all) for p in parts]
    num = sum(w * p[2] for w, p in zip(ws, parts))
    den = sum(w * p[1] for w, p in zip(ws, parts))
    return num / den


ATT_QSUB = 64


def _band_mask(nq, nk, shift, window):
    dist = (lax.broadcasted_iota(jnp.int32, (nq, nk), 0) + shift) - lax.broadcasted_iota(jnp.int32, (nq, nk), 1)
    return (dist >= 0) & (dist <= window)
def _attn_prompt_kernel(aq_ref, ak_ref, av_ref, cos_ref, sin_ref, qg_ref, kg_ref,
                        ob_ref, kout_ref, vout_ref, q_s, k_s, qd_s, kd_s, vd_s, o_s, m_s, l_s):
    L = q_s.shape[0]
    blk = ATT_BLOCK
    rows = 256

    def prep(i, carry):
        r = pl.multiple_of(i * rows, rows)
        sl = pl.ds(r, rows)
        cos2, sin2 = cos_ref[sl, :], sin_ref[sl, :]
        q = _norm_rot(aq_ref[0, sl, :], qg_ref[...], cos2, sin2) * ATTN_SCALE
        k = _norm_rot(ak_ref[0, sl, :], kg_ref[...], cos2, sin2)
        v = av_ref[0, sl, :]
        q_s[sl, :] = q
        k_s[sl, :] = k
        kout_ref[0, sl, :] = k
        vout_ref[0, sl, :] = v
        qd_s[0, sl, :] = q.astype(BF16)
        kd_s[0, sl, :] = k.astype(BF16)
        vd_s[0, sl, :] = v.astype(BF16)
        return carry

    lax.fori_loop(0, L // rows, prep, 0)

    for branch, (window, dil) in enumerate(DIL_PAIRS):
        assert window // dil == blk
        n = L // dil
        for r in range(dil if dil > 1 else 0):
            src = pl.ds(r, n, stride=dil)
            dst = pl.ds(r * n, n)
            qd_s[branch, dst, :] = q_s[src, :].astype(BF16)
            kd_s[branch, dst, :] = k_s[src, :].astype(BF16)
            vd_s[branch, dst, :] = av_ref.at[0][src, :].astype(BF16)

    qs = ATT_QSUB
    band_mask = _band_mask(qs, qs + blk, blk, blk)
    first_masks = [_band_mask(qs, qo + qs, qo, blk) for qo in range(0, blk, qs)]

    def tiles(dil, r, i, first):
        n = L // dil
        out = []
        for qo in range(0, blk, qs):
            q0 = r * n + i * blk + qo
            k0, nk, mask = (q0 - qo, qo + qs, first_masks[qo // qs]) if first else (q0 - blk, qs + blk, band_mask)
            if not isinstance(q0, int):
                q0, k0 = pl.multiple_of(q0, qs), pl.multiple_of(k0, qs)
            out.append((q0, k0, nk, mask, pl.ds(r + dil * (blk * i + qo), qs, stride=dil if dil > 1 else None)))
        return out

    def run(branch, blocks):
        ts = [t for blk_args in blocks for t in tiles(*blk_args)]
        s = [_dot_nt(qd_s[branch, pl.ds(q0, qs), :], kd_s[branch, pl.ds(k0, nk), :], preferred_element_type=F32)
             for q0, k0, nk, _, _ in ts]
        s = [jnp.where(t[3], x, -jnp.inf) for t, x in zip(ts, s)]
        m = [jnp.max(x, -1, keepdims=True) for x in s]
        p = [jnp.exp(x - mx) for x, mx in zip(s, m)]
        l = [jnp.sum(x, -1, keepdims=True) for x in p]
        o = [jnp.dot(x.astype(BF16), vd_s[branch, pl.ds(t[1], t[2]), :], preferred_element_type=F32)
             for t, x in zip(ts, p)]
        for t, mx, lx, ox in zip(ts, m, l, o):
            o_s.at[branch][t[4], :] = ox
            m_s.at[branch][t[4], :] = jnp.broadcast_to(mx, (qs, HEAD_DIM))
            l_s.at[branch][t[4], :] = jnp.broadcast_to(lx, (qs, HEAD_DIM))

    group = 5
    for branch, (window, dil) in enumerate(DIL_PAIRS):
        nb = L // dil // blk
        if dil == 1:
            assert (nb - 1) % group == 0
            run(branch, [(dil, 0, 0, True)])

            def later(g, carry, branch=branch, dil=dil):
                run(branch, [(dil, 0, 1 + g * group + u, False) for u in range(group)])
                return carry

            lax.fori_loop(0, (nb - 1) // group, later, 0)
        else:
            per_body = max(1, 8 // nb)

            def residues(g, carry, branch=branch, dil=dil, nb=nb, per_body=per_body):
                run(branch, [(dil, g * per_body + u, i, i == 0) for u in range(per_body) for i in range(nb)])
                return carry

            lax.fori_loop(0, dil // per_body, residues, 0)

    def merge(i, carry):
        sl = pl.ds(pl.multiple_of(i * rows, rows), rows)
        parts = [(m_s[b, sl, :], l_s[b, sl, :], o_s[b, sl, :]) for b in range(len(DIL_PAIRS))]
        ob_ref[0, sl, :] = _merge(parts).astype(ob_ref.dtype)
        return carry

    lax.fori_loop(0, L // rows, merge, 0)


def _attn_prompt(z3, cos2, sin2, q_norm_g, k_norm_g):
    B, L, _ = z3.shape
    nbr = len(DIL_PAIRS)
    head_blk = lambda c0: pl.BlockSpec((1, L, HEAD_DIM), lambda b, h, c0=c0: (b, 0, c0 + h))
    out_blk = pl.BlockSpec((1, L, HEAD_DIM), lambda b, h: (b, 0, h))
    tab = pl.BlockSpec((L, HEAD_DIM), lambda b, h: (0, 0))
    vec = pl.BlockSpec((1, HEAD_DIM), lambda b, h: (0, 0))
    blocks = 5 * _nbytes((L, HEAD_DIM), F32) + 2 * _nbytes((L, HEAD_DIM), F32) + _nbytes((L, HEAD_DIM), BF16)
    scratch_bytes = (2 + 3 * nbr) * _nbytes((L, HEAD_DIM), F32) + 3 * nbr * _nbytes((L, HEAD_DIM), BF16)
    return pl.pallas_call(
        _attn_prompt_kernel,
        grid=(B, H_ATTN),
        in_specs=[head_blk(COL_AQ), head_blk(COL_AK), head_blk(COL_AV), tab, tab, vec, vec],
        out_specs=[out_blk, out_blk, out_blk],
        out_shape=[jax.ShapeDtypeStruct((B, L, W_ATTN), BF16),
                   jax.ShapeDtypeStruct((B, L, W_ATTN), F32),
                   jax.ShapeDtypeStruct((B, L, W_ATTN), F32)],
        scratch_shapes=[pltpu.VMEM((L, HEAD_DIM), F32), pltpu.VMEM((L, HEAD_DIM), F32),
                        pltpu.VMEM((nbr, L, HEAD_DIM), BF16), pltpu.VMEM((nbr, L, HEAD_DIM), BF16),
                        pltpu.VMEM((nbr, L, HEAD_DIM), BF16),
                        pltpu.VMEM((nbr, L, HEAD_DIM), F32), pltpu.VMEM((nbr, L, HEAD_DIM), F32),
                        pltpu.VMEM((nbr, L, HEAD_DIM), F32)],
        compiler_params=pltpu.CompilerParams(
            dimension_semantics=("parallel", "parallel"),
            vmem_limit_bytes=_vmem_limit(blocks, scratch_bytes)),
        name="attn_prompt",
    )(z3, z3, z3, cos2, sin2, q_norm_g, k_norm_g)


def _attn_sample_kernel(aq_ref, ak_ref, av_ref, ck_ref, cv_ref, ckh_ref, cvh_ref, cos_ref, sin_ref, qg_ref, kg_ref,
                        ob_ref, wk_ref, wv_ref, q_s, kn_s, m_s, l_s, acc_s):
    T = aq_ref.shape[1]
    H = H_ATTN
    rows = ck_ref.shape[1]
    P = rows // H
    j = pl.program_id(1)
    nj = pl.num_programs(1)
    wbuf = P * nj
    shift = T * H
    nbr = len(DIL_PAIRS)
    head_lanes = lambda h: slice(h * HEAD_DIM, (h + 1) * HEAD_DIM)

    @pl.when(j == 0)
    def _():
        cos2, sin2 = cos_ref[...], sin_ref[...]
        for h in range(H):
            q_s[h] = _norm_rot(aq_ref[0, :, head_lanes(h)], qg_ref[...], cos2, sin2)
            kn_s[h] = _norm_rot(ak_ref[0, :, head_lanes(h)], kg_ref[...], cos2, sin2)
        m_s[...] = jnp.full(m_s.shape, -jnp.inf, F32)
        l_s[...] = jnp.zeros(l_s.shape, F32)
        acc_s[...] = jnp.zeros(acc_s.shape, F32)

    def update(br, h, s, mask, v):
        sm = jnp.where(mask, s, -jnp.inf)
        m_old = m_s[br, h]
        m_new = jnp.maximum(m_old, jnp.max(sm, -1, keepdims=True))
        m_safe = jnp.where(m_new == -jnp.inf, 0.0, m_new)
        p = jnp.exp(sm - m_safe)
        alpha = jnp.exp(m_old - m_safe)
        l_s[br, h] = alpha * l_s[br, h] + jnp.sum(p, -1, keepdims=True)
        acc_s[br, h] = alpha * acc_s[br, h] + jnp.dot(p.astype(BF16), v.astype(BF16), preferred_element_type=F32)
        m_s[br, h] = m_new

    t_c = lax.broadcasted_iota(jnp.int32, (T, P), 0)
    dist_c = wbuf + t_c - (j * P + lax.broadcasted_iota(jnp.int32, (T, P), 1))
    for h in range(H):
        k_h = ck_ref.at[0][pl.ds(h, P, stride=H), :]
        v_h = cv_ref.at[0][pl.ds(h, P, stride=H), :]
        s = _dot_nt(q_s[h].astype(BF16), k_h.astype(BF16), preferred_element_type=F32) * ATTN_SCALE
        for br, (window, dil) in enumerate(DIL_PAIRS):
            update(br, h, s, ((dist_c & (dil - 1)) == 0) & (dist_c <= window), v_h)

    wk_ref[0, 0:rows - shift, :] = ck_ref[0, shift:rows, :]
    wv_ref[0, 0:rows - shift, :] = cv_ref[0, shift:rows, :]

    @pl.when(j < nj - 1)
    def _():
        wk_ref[0, rows - shift:rows, :] = ckh_ref[0]
        wv_ref[0, rows - shift:rows, :] = cvh_ref[0]

    @pl.when(j == nj - 1)
    def _():
        t_n = lax.broadcasted_iota(jnp.int32, (T, T), 0)
        dist_n = t_n - lax.broadcasted_iota(jnp.int32, (T, T), 1)
        for h in range(H):
            k_new = kn_s[h]
            v_new = av_ref[0, :, head_lanes(h)]
            s = _dot_nt(q_s[h].astype(BF16), k_new.astype(BF16), preferred_element_type=F32) * ATTN_SCALE
            for br, (window, dil) in enumerate(DIL_PAIRS):
                update(br, h, s, (dist_n >= 0) & ((dist_n & (dil - 1)) == 0) & (dist_n <= window), v_new)
            parts = [(m_s[br, h], l_s[br, h], acc_s[br, h]) for br in range(nbr)]
            ob_ref[0, :, head_lanes(h)] = _merge(parts).astype(ob_ref.dtype)
            wk_ref.at[0][pl.ds(rows - shift + h, T, stride=H), :] = k_new
            wv_ref.at[0][pl.ds(rows - shift + h, T, stride=H), :] = v_new


def _attn_sample(z3, cache_k, cache_v, cos2, sin2, q_norm_g, k_norm_g, pos_chunk=512):
    B, T, _ = z3.shape
    rows_all = cache_k.shape[1]
    rows = pos_chunk * H_ATTN
    nj = rows_all // rows
    shift = T * H_ATTN
    per_chunk = rows // shift
    zblk = lambda c0: pl.BlockSpec((1, T, W_ATTN), lambda b, j, c0=c0: (b, 0, c0 // H_ATTN))
    cblk = pl.BlockSpec((1, rows, HEAD_DIM), lambda b, j: (b, j, 0))
    halo = pl.BlockSpec((1, shift, HEAD_DIM), lambda b, j: (b, jnp.minimum((j + 1) * per_chunk, nj * per_chunk - 1), 0))
    tab = pl.BlockSpec((T, HEAD_DIM), lambda b, j: (0, 0))
    vec = pl.BlockSpec((1, HEAD_DIM), lambda b, j: (0, 0))
    nbr = len(DIL_PAIRS)
    blocks = 4 * _nbytes((rows, HEAD_DIM), F32) + 2 * _nbytes((shift, HEAD_DIM), F32) + 4 * _nbytes((T, W_ATTN), F32)
    scratch_dims = [(H_ATTN, T, HEAD_DIM), (H_ATTN, T, HEAD_DIM), (nbr, H_ATTN, T, 1), (nbr, H_ATTN, T, 1),
                    (nbr, H_ATTN, T, HEAD_DIM)]
    return pl.pallas_call(
        _attn_sample_kernel,
        grid=(B, nj),
        in_specs=[zblk(COL_AQ), zblk(COL_AK), zblk(COL_AV), cblk, cblk, halo, halo, tab, tab, vec, vec],
        out_specs=[pl.BlockSpec((1, T, W_ATTN), lambda b, j: (b, 0, 0)), cblk, cblk],
        out_shape=[jax.ShapeDtypeStruct((B, T, W_ATTN), BF16),
                   jax.ShapeDtypeStruct((B, rows_all, HEAD_DIM), F32),
                   jax.ShapeDtypeStruct((B, rows_all, HEAD_DIM), F32)],
        scratch_shapes=[pltpu.VMEM(d, F32) for d in scratch_dims],
        compiler_params=pltpu.CompilerParams(
            dimension_semantics=("parallel", "arbitrary"),
            vmem_limit_bytes=_vmem_limit(blocks, sum(_nbytes(d, F32) for d in scratch_dims))),
        name="attn_sample",
    )(z3, z3, z3, cache_k, cache_v, cache_k, cache_v, cos2, sin2, q_norm_g, k_norm_g)


def _route(logits):
    lane = lax.broadcasted_iota(jnp.int32, logits.shape, 1).astype(F32)
    first_max = lambda p, top, ok: jnp.min(jnp.where(ok & (p == top), lane, float(LANES)), -1, keepdims=True)
    is_group = lane < N_GROUPS
    gl = jnp.where(is_group, logits, -jnp.inf)
    pg = jnp.exp(gl - jnp.max(gl, -1, keepdims=True))
    pg = pg / jnp.sum(pg, -1, keepdims=True)
    pg_top = jnp.max(pg, -1, keepdims=True)
    g_top = first_max(pg, pg_top, is_group)
    lo = N_GROUPS + g_top * EXPERTS_PER_GROUP
    in_group = (lane >= lo) & (lane < lo + EXPERTS_PER_GROUP)
    el = jnp.where(in_group, logits, -jnp.inf)
    pe = jnp.exp(el - jnp.max(el, -1, keepdims=True))
    pe = pe / jnp.sum(pe, -1, keepdims=True)
    p1 = jnp.max(pe, -1, keepdims=True)
    l1 = first_max(pe, p1, in_group)
    rest = in_group & (lane != l1)
    p2 = jnp.max(jnp.where(rest, pe, -1.0), -1, keepdims=True)
    l2 = first_max(pe, p2, rest)
    denom = p1 + p2
    gates = jnp.where(lane == 0.0, pg_top * p1 / denom, jnp.where(lane == 1.0, pg_top * p2 / denom, 0.0))
    experts = jnp.where(lane == 0.0, l1 - N_GROUPS, jnp.where(lane == 1.0, l2 - N_GROUPS, 0.0))
    return gates, experts.astype(jnp.int32)


def _out_proj_kernel(oa_ref, ob_ref, x_ref, wa_ref, wb_ref, g_ref, wrh_ref, wrl_ref, br_ref, h_ref, gate_ref, exp_ref):
    h = (x_ref[...] + jnp.dot(oa_ref[...], wa_ref[...], preferred_element_type=F32)
         + jnp.dot(ob_ref[...], wb_ref[...], preferred_element_type=F32))
    h_ref[...] = h
    hn_hi, hn_lo = _split_bf16(_rms(h, g_ref[...]))
    w_hi, w_lo = wrh_ref[...], wrl_ref[...]
    logits = (jnp.dot(hn_hi, w_hi, preferred_element_type=F32) + jnp.dot(hn_hi, w_lo, preferred_element_type=F32)
              + jnp.dot(hn_lo, w_hi, preferred_element_type=F32)) + br_ref[...]
    gate_ref[...], exp_ref[...] = _route(logits)


def _out_proj(o_a, o_b, x2d, w_a, w_b, g, w_rt_hi, w_rt_lo, b_rt, tm):
    T = x2d.shape[0]
    row = lambda w: pl.BlockSpec((tm, w), lambda i: (i, 0))
    full = lambda shape: pl.BlockSpec(shape, lambda i: (0, 0))
    blocks = (2 * _nbytes((tm, W_DELTA), BF16) + 2 * _nbytes((tm, D_MODEL), F32) + 2 * _nbytes((W_DELTA, D_MODEL), BF16)
              + 2 * _nbytes((D_MODEL, LANES), BF16) + 2 * _nbytes((tm, LANES), F32))
    return pl.pallas_call(
        _out_proj_kernel,
        grid=(T // tm,),
        in_specs=[row(W_DELTA), row(W_ATTN), row(D_MODEL), full((W_DELTA, D_MODEL)), full((W_ATTN, D_MODEL)),
                  full((1, D_MODEL)), full((D_MODEL, LANES)), full((D_MODEL, LANES)), full((1, LANES))],
        out_specs=[row(D_MODEL), row(LANES), row(LANES)],
        out_shape=[jax.ShapeDtypeStruct((T, D_MODEL), F32), jax.ShapeDtypeStruct((T, LANES), F32),
                   jax.ShapeDtypeStruct((T, LANES), jnp.int32)],
        compiler_params=pltpu.CompilerParams(
            dimension_semantics=("parallel",), vmem_limit_bytes=_vmem_limit(blocks)),
        name="out_proj",
    )(o_a, o_b, x2d, w_a, w_b, g, w_rt_hi, w_rt_lo, b_rt)


def _row_gather(idx_ref, n, src_hbm, dst, sem):
    for r in range(n):
        pltpu.make_async_copy(src_hbm.at[pl.ds(idx_ref[0, 0, r], 1)], dst.at[pl.ds(r, 1)], sem).start()


def _expert_kernel(be_ref, nv_ref, tok_ref, tok_next_ref, h_hbm, g_ref, wg_ref, wu_ref, wd_ref, y_ref, xbuf, sem):
    del be_ref
    tm = xbuf.shape[1]
    i = pl.program_id(0)
    n_valid = nv_ref[0]
    slot = lax.rem(i, 2)

    @pl.when((i == 0) & (n_valid > 0))
    def _():
        _row_gather(tok_ref, tm, h_hbm, xbuf.at[0], sem.at[0])

    @pl.when(i + 1 < n_valid)
    def _():
        _row_gather(tok_next_ref, tm, h_hbm, xbuf.at[1 - slot], sem.at[1 - slot])

    @pl.when(i < n_valid)
    def _():
        pltpu.make_async_copy(h_hbm.at[pl.ds(0, tm)], xbuf.at[slot], sem.at[slot]).wait()
        x = _rms(xbuf[slot], g_ref[...]).astype(BF16)
        a = jnp.dot(x, wg_ref[0], preferred_element_type=F32)
        b = jnp.dot(x, wu_ref[0], preferred_element_type=F32)
        hmid = (a * _sigmoid(a)) * b
        y_ref[...] = jnp.dot(hmid.astype(BF16), wd_ref[0], preferred_element_type=F32)

    @pl.when(i >= n_valid)
    def _():
        y_ref[...] = jnp.zeros(y_ref.shape, F32)


def _experts(block_e, n_valid, rows_tok, h, g, w_g, w_u, w_d, tm):
    nb = rows_tok.shape[0]
    last = lambda i, nv: jnp.minimum(i, jnp.maximum(nv[0] - 1, 0))
    smem_blk = lambda f: pl.BlockSpec((1, 1, tm), f, memory_space=pltpu.SMEM)
    blocks = 3 * _nbytes((D_MODEL, EXPERT_FF), BF16) + _nbytes((tm, D_MODEL), F32)
    scratch_bytes = 2 * _nbytes((tm, D_MODEL), F32)
    return pl.pallas_call(
        _expert_kernel,
        grid_spec=pltpu.PrefetchScalarGridSpec(
            num_scalar_prefetch=2,
            grid=(nb,),
            in_specs=[
                smem_blk(lambda i, be, nv: (last(i, nv), 0, 0)),
                smem_blk(lambda i, be, nv: (last(i + 1, nv), 0, 0)),
                pl.BlockSpec(memory_space=pl.ANY),
                pl.BlockSpec((1, D_MODEL), lambda i, be, nv: (0, 0)),
                pl.BlockSpec((1, D_MODEL, EXPERT_FF), lambda i, be, nv: (be[last(i, nv)], 0, 0)),
                pl.BlockSpec((1, D_MODEL, EXPERT_FF), lambda i, be, nv: (be[last(i, nv)], 0, 0)),
                pl.BlockSpec((1, EXPERT_FF, D_MODEL), lambda i, be, nv: (be[last(i, nv)], 0, 0)),
            ],
            out_specs=pl.BlockSpec((tm, D_MODEL), lambda i, be, nv: (i, 0)),
            scratch_shapes=[pltpu.VMEM((2, tm, D_MODEL), F32), pltpu.SemaphoreType.DMA((2,))],
        ),
        out_shape=jax.ShapeDtypeStruct((nb * tm, D_MODEL), F32),
        compiler_params=pltpu.CompilerParams(
            dimension_semantics=("arbitrary",), vmem_limit_bytes=_vmem_limit(blocks, scratch_bytes)),
        name="experts",
    )(block_e, n_valid, rows_tok, rows_tok, h, g, w_g, w_u, w_d)


def _combine_kernel(dst_ref, dst_next_ref, yb_hbm, h_ref, gate_ref, y_ref, ybuf, sem):
    tt = h_ref.shape[0]
    n = TOP_K * tt
    i = pl.program_id(0)
    slot = lax.rem(i, 2)

    @pl.when(i == 0)
    def _():
        _row_gather(dst_ref, n, yb_hbm, ybuf.at[0], sem.at[0])

    @pl.when(i + 1 < pl.num_programs(0))
    def _():
        _row_gather(dst_next_ref, n, yb_hbm, ybuf.at[1 - slot], sem.at[1 - slot])

    pltpu.make_async_copy(yb_hbm.at[pl.ds(0, n)], ybuf.at[slot], sem.at[slot]).wait()
    gates = gate_ref[...]
    moe = gates[:, 0:1] * ybuf[slot, 0:tt, :]
    for k in range(1, TOP_K):
        moe = moe + gates[:, k:k + 1] * ybuf[slot, k * tt:(k + 1) * tt, :]
    y_ref[...] = h_ref[...] + moe


def _combine(dest_blocks, yb, h, gates, tt):
    T = h.shape[0]
    n = TOP_K * tt
    nblk = T // tt
    smem_blk = lambda f: pl.BlockSpec((1, 1, n), f, memory_space=pltpu.SMEM)
    blocks = 2 * _nbytes((tt, D_MODEL), F32) + _nbytes((tt, LANES), F32)
    scratch_bytes = 2 * _nbytes((n, D_MODEL), F32)
    return pl.pallas_call(
        _combine_kernel,
        grid=(nblk,),
        in_specs=[smem_blk(lambda i: (i, 0, 0)), smem_blk(lambda i: (jnp.minimum(i + 1, nblk - 1), 0, 0)),
                  pl.BlockSpec(memory_space=pl.ANY),
                  pl.BlockSpec((tt, D_MODEL), lambda i: (i, 0)), pl.BlockSpec((tt, LANES), lambda i: (i, 0))],
        out_specs=pl.BlockSpec((tt, D_MODEL), lambda i: (i, 0)),
        out_shape=jax.ShapeDtypeStruct((T, D_MODEL), F32),
        scratch_shapes=[pltpu.VMEM((2, n, D_MODEL), F32), pltpu.SemaphoreType.DMA((2,))],
        compiler_params=pltpu.CompilerParams(
            dimension_semantics=("arbitrary",), vmem_limit_bytes=_vmem_limit(blocks, scratch_bytes)),
        name="combine",
    )(dest_blocks, dest_blocks, yb, h, gates)


def _moe(h, gates, experts, g, w_g, w_u, w_d, tm, tt):
    T = h.shape[0]
    A = T * TOP_K
    flat_e = experts[:, :TOP_K].reshape(A)
    flat_tok = jnp.repeat(jnp.arange(T, dtype=jnp.int32), TOP_K)
    order = jnp.argsort(flat_e)
    se, stok = flat_e[order], flat_tok[order]
    counts = jnp.bincount(flat_e, length=N_EXPERTS)
    starts = jnp.cumsum(counts) - counts
    pcounts = (counts + tm - 1) // tm * tm
    pends = jnp.cumsum(pcounts)
    pstarts = pends - pcounts
    dest_sorted = (pstarts[se] + jnp.arange(A) - starts[se]).astype(jnp.int32)
    nb = -(-A // tm) + N_EXPERTS
    block_e = jnp.minimum(jnp.sum(pends[None, :] <= (jnp.arange(nb) * tm)[:, None], -1), N_EXPERTS - 1).astype(jnp.int32)
    first = jnp.clip(starts[block_e] + jnp.arange(nb) * tm - pstarts[block_e], 0, A).astype(jnp.int32)
    stok_pad = jnp.concatenate([stok, jnp.zeros((tm,), jnp.int32)])
    rows_tok = jax.vmap(lambda s: lax.dynamic_slice(stok_pad, (s,), (tm,)))(first)
    n_valid = (pends[-1:] // tm).astype(jnp.int32)
    yb = _experts(block_e, n_valid, rows_tok.reshape(nb, 1, tm), h, g, w_g, w_u, w_d, tm)
    dest = dest_sorted[jnp.argsort(order)].reshape(T // tt, tt, TOP_K)
    dest_blocks = dest.transpose(0, 2, 1).reshape(T // tt, 1, TOP_K * tt)
    return _combine(dest_blocks, yb, h, gates, tt)


def _rope_tables(pos):
    half = HEAD_DIM // 2
    inv = ROPE_THETA ** (-jnp.arange(half, dtype=F32) / half)
    ang = pos.astype(F32)[:, None] * inv[None, :]
    cos, sin = jnp.cos(ang), jnp.sin(ang)
    return jnp.concatenate([cos, cos], -1), jnp.concatenate([-sin, sin], -1)


def _layer(x, pos, w, tm, lt, chunk, moe_tm, moe_tt, conv_prev=None, s0=None, win_k=None, win_v=None):
    B, L, D = x.shape
    T = B * L
    x2d = x.reshape(T, D)
    z, zs = _in_proj(x2d, w['norm1_g'], w['w_main'], w['w_small'], min(2 * tm, T))
    z3 = z.reshape(B, L, MAIN_COLS)
    zs3 = zs.reshape(B, L, LANES)
    if conv_prev is None:
        o_a, s_new, conv_new = _delta_prompt(z3, zs3, w['conv_w'], w['a_vec'], w['dt_vec'], w['o_norm_g'], lt)
    else:
        o_a, s_new, conv_new = _delta_mixer(z3, zs3, w['conv_w'], w['a_vec'], w['dt_vec'], w['o_norm_g'], lt, chunk,
                                            conv_prev, s0)
    cos2, sin2 = _rope_tables(pos)
    if win_k is None:
        o_b, wk_new, wv_new = _attn_prompt(z3, cos2, sin2, w['q_norm_g'], w['k_norm_g'])
    else:
        rows = win_k.shape[1] * H_ATTN
        o_b, wk_new, wv_new = _attn_sample(z3, win_k.reshape(B, rows, HEAD_DIM), win_v.reshape(B, rows, HEAD_DIM),
                                           cos2, sin2, w['q_norm_g'], w['k_norm_g'])
    h, gates, experts = _out_proj(o_a.reshape(T, W_DELTA), o_b.reshape(T, W_ATTN), x2d, w['w_out_a'], w['w_out_b'],
                                  w['norm2_g'], w['w_rt_hi'], w['w_rt_lo'], w['b_rt'], tm)
    y = _moe(h, gates, experts, w['norm2_g'], w['w_g'], w['w_u'], w['w_d'], moe_tm, moe_tt)
    keep = wk_new.size // (B * W_ATTN)
    return (y.reshape(B, L, D), conv_new, s_new,
            wk_new.reshape(B, keep, H_ATTN, HEAD_DIM), wv_new.reshape(B, keep, H_ATTN, HEAD_DIM))


def _prep_weights(norm1_g, w_in, conv_w, A_log, dt_bias, o_norm_g, q_norm_g, k_norm_g, w_out, norm2_g,
                  w_group, b_group, w_router, b_router, w_exp_gate, w_exp_up, w_exp_down):
    sizes = [W_DELTA, W_DELTA, W_DELTA, H_DELTA, H_DELTA, W_DELTA, W_ATTN, W_ATTN, W_ATTN]
    zq, zk, zv, zb, za, zg, aq, ak, av = jnp.split(w_in, np.cumsum(sizes)[:-1].tolist(), axis=-1)
    pad = LANES - 2 * H_DELTA
    lane_vec = lambda v: jnp.pad(v.astype(F32), (H_DELTA, LANES - 2 * H_DELTA))[None, :]
    w_rt = jnp.pad(jnp.concatenate([w_group, w_router], -1), ((0, 0), (0, LANES - N_GROUPS - N_EXPERTS)))
    w_rt_hi = w_rt.astype(BF16)
    return dict(
        norm1_g=norm1_g[None, :],
        w_main=jnp.concatenate([zq, zk, zv, zg, aq, ak, av], -1).astype(BF16),
        w_small=jnp.pad(jnp.concatenate([zb, za], -1), ((0, 0), (0, pad))).astype(BF16),
        conv_w=conv_w,
        a_vec=lane_vec(A_log), dt_vec=lane_vec(dt_bias),
        o_norm_g=o_norm_g[None, :], q_norm_g=q_norm_g[None, :], k_norm_g=k_norm_g[None, :],
        w_out_a=w_out[:W_DELTA].astype(BF16), w_out_b=w_out[W_DELTA:].astype(BF16),
        norm2_g=norm2_g[None, :],
        w_rt_hi=w_rt_hi, w_rt_lo=(w_rt - w_rt_hi.astype(F32)).astype(BF16),
        b_rt=jnp.pad(jnp.concatenate([b_group, b_router], -1), (0, LANES - N_GROUPS - N_EXPERTS))[None, :],
        w_g=w_exp_gate.astype(BF16), w_u=w_exp_up.astype(BF16), w_d=w_exp_down.astype(BF16),
    )


def kernel(x_prompt, x_sample, state_conv, state_delta, cache_win_k, cache_win_v, norm1_g, w_in, conv_w, A_log, dt_bias, o_norm_g, q_norm_g, k_norm_g, w_out, norm2_g, w_group, b_group, w_router, b_router, w_exp_gate, w_exp_up, w_exp_down):
    depth = w_in.shape[0]
    pos_p = jnp.arange(x_prompt.shape[1], dtype=jnp.int32)
    pos_s = PAST_LEN + jnp.arange(x_sample.shape[1], dtype=jnp.int32)
    yp, ys = x_prompt, x_sample
    outs = [[] for _ in range(8)]
    for l in range(depth):
        w = _prep_weights(norm1_g[l], w_in[l], conv_w[l], A_log[l], dt_bias[l], o_norm_g[l], q_norm_g[l],
                          k_norm_g[l], w_out[l], norm2_g[l], w_group[l], b_group[l], w_router[l], b_router[l],
                          w_exp_gate[l], w_exp_up[l], w_exp_down[l])
        yp, c1, s1, k1, v1 = _layer(yp, pos_p, w, tm=512, lt=256, chunk=CHUNK, moe_tm=512, moe_tt=256)
        ls = x_sample.shape[1]
        ys, c2, s2, k2, v2 = _layer(ys, pos_s, w, tm=x_sample.shape[0] * ls, lt=ls, chunk=min(CHUNK, ls), moe_tm=128,
                                    moe_tt=128,
                                    conv_prev=state_conv[l], s0=state_delta[l],
                                    win_k=cache_win_k[l], win_v=cache_win_v[l])
        for lst, v in zip(outs, (c1, c2, s1, s2, k1, k2, v1, v2)):
            lst.append(v)
    return (yp, ys) + tuple(jnp.stack(o) for o in outs)
```

```python
import functools
import math

import numpy as np
import jax
import jax.numpy as jnp
from jax import lax
from jax.experimental import pallas as pl
from jax.experimental.pallas import tpu as pltpu

F32 = jnp.float32
BF16 = jnp.bfloat16
HIGHEST = lax.Precision.HIGHEST

D_MODEL = 2048
HEAD_DIM = 128
H_DELTA = 8
H_ATTN = 8
W_DELTA = H_DELTA * HEAD_DIM
W_ATTN = H_ATTN * HEAD_DIM
CONV_W = 4
CHUNK = 64
DIL_PAIRS = ((128, 1), (512, 4), (2048, 16))
ATT_BLOCK = 128
ROPE_THETA = 10000.0
N_GROUPS = 4
EXPERTS_PER_GROUP = 8
N_EXPERTS = N_GROUPS * EXPERTS_PER_GROUP
TOP_K = 2
EXPERT_FF = 768
EPS = 1e-6
PAST_LEN = 16384

LANES = 128
SUBLANES = 8
V7X_VMEM_BYTES = 64 * 1024 * 1024
COMPILER_TEMP_BYTES = 12 * 1024 * 1024

COL_ZQ, COL_ZK, COL_ZV, COL_ZG, COL_AQ, COL_AK, COL_AV = (i * H_DELTA for i in range(7))
MAIN_COLS = 7 * W_DELTA
ATTN_SCALE = HEAD_DIM ** -0.5


def _vmem_limit(block_bytes, scratch_bytes=0):
    return int(min(2 * block_bytes + scratch_bytes + COMPILER_TEMP_BYTES, V7X_VMEM_BYTES - 4 * 1024 * 1024))


def _nbytes(shape, dtype):
    return int(np.prod(shape)) * jnp.dtype(dtype).itemsize


def _rms(x, g):
    return x * lax.rsqrt(jnp.mean(x * x, -1, keepdims=True) + EPS) * g


def _sigmoid(x):
    return 1.0 / (1.0 + jnp.exp(-x))


def _dot_nt(a, b, **kw):
    return lax.dot_general(a, b, (((1,), (1,)), ((), ())), **kw)


def _dot_tn(a, b, **kw):
    return lax.dot_general(a, b, (((0,), (0,)), ((), ())), **kw)


def _in_proj_kernel(x_ref, g_ref, w_ref, ws_ref, z_ref, zs_ref, xn_ref):
    @pl.when(pl.program_id(1) == 0)
    def _():
        xn_ref[...] = _rms(x_ref[...], g_ref[...]).astype(BF16)
        zs_ref[...] = jnp.dot(xn_ref[...], ws_ref[...], preferred_element_type=F32)

    z_ref[...] = jnp.dot(xn_ref[...], w_ref[...], preferred_element_type=F32)


def _in_proj(x2d, g, w_main, w_small, tm, tn=1024):
    T = x2d.shape[0]
    blocks = (_nbytes((tm, D_MODEL), F32) + _nbytes((D_MODEL, tn), BF16) + _nbytes((D_MODEL, LANES), BF16)
              + _nbytes((tm, tn), F32) + _nbytes((tm, LANES), F32))
    return pl.pallas_call(
        _in_proj_kernel,
        grid=(T // tm, MAIN_COLS // tn),
        in_specs=[
            pl.BlockSpec((tm, D_MODEL), lambda i, j: (i, 0)),
            pl.BlockSpec((1, D_MODEL), lambda i, j: (0, 0)),
            pl.BlockSpec((D_MODEL, tn), lambda i, j: (0, j)),
            pl.BlockSpec((D_MODEL, LANES), lambda i, j: (0, 0)),
        ],
        out_specs=[
            pl.BlockSpec((tm, tn), lambda i, j: (i, j)),
            pl.BlockSpec((tm, LANES), lambda i, j: (i, 0)),
        ],
        out_shape=[jax.ShapeDtypeStruct((T, MAIN_COLS), F32), jax.ShapeDtypeStruct((T, LANES), F32)],
        scratch_shapes=[pltpu.VMEM((tm, D_MODEL), BF16)],
        compiler_params=pltpu.CompilerParams(
            dimension_semantics=("parallel", "arbitrary"),
            vmem_limit_bytes=_vmem_limit(blocks, _nbytes((tm, D_MODEL), BF16))),
        name="in_proj",
    )(x2d, g, w_main, w_small)


def _delta_chunk(cq, ck, cv, beta_b, g_b, S):
    C = cq.shape[0]
    q = cq * lax.rsqrt(jnp.sum(cq * cq, -1, keepdims=True) + EPS) * ATTN_SCALE
    k = ck * lax.rsqrt(jnp.sum(ck * ck, -1, keepdims=True) + EPS)
    row = lax.broadcasted_iota(jnp.int32, (C, C), 0)
    col = lax.broadcasted_iota(jnp.int32, (C, C), 1)
    eye = jnp.where(row == col, 1.0, 0.0)
    gam_b = _dot_sel(jnp.where(row >= col, 1.0, 0.0).astype(BF16), g_b)
    gam_col = gam_b[:, :C]
    gam_row = _dot_sel(jnp.ones((C, C), BF16), eye * gam_col)
    dec = jnp.exp(jnp.minimum(gam_col - gam_row, 0.0))
    dec_strict = jnp.where(row > col, dec, 0.0)
    dec_incl = jnp.where(row >= col, dec, 0.0)
    kq = _dot3(jnp.concatenate([k, q], 0), k, dot=_dot_nt)
    qk = kq[C:] * dec_incl
    x = -(beta_b[:, :C] * kq[:C] * dec_strict)
    inv = eye + x
    for _ in range(int(math.log2(C)) - 1):
        x = _dot3(x, x)
        inv = inv + _dot3(inv, x)
    eg = jnp.exp(gam_b)
    uw = _dot3(inv, jnp.concatenate([beta_b * cv, beta_b * eg * k], 1))
    u, w = uw[:, :HEAD_DIM], uw[:, HEAD_DIM:]
    gam_last = gam_b[C - 1:C, :]
    kdec = k * jnp.exp(gam_last - gam_b)
    wq_s = _dot3(jnp.concatenate([w, q], 0), S, dot=_dot_nt)
    delta = u - wq_s[:C]
    o = eg * wq_s[C:] + _dot3(qk, delta)
    s_new = jnp.exp(gam_last) * S + _dot_tn(delta, kdec, precision=HIGHEST, preferred_element_type=F32)
    return o, s_new


def _delta_kernel(*refs, lt, chunk, has_state):
    if has_state:
        (zq_ref, zk_ref, zv_ref, zg_ref, zs_ref, cw_ref, a_ref, dt_ref, og_ref, prev_ref, s0_ref,
         o_ref, sout_ref, cout_ref, u_s, c_s, s_s) = refs
    else:
        (zq_ref, zk_ref, zv_ref, zg_ref, zs_ref, cw_ref, a_ref, dt_ref, og_ref,
         o_ref, sout_ref, cout_ref, u_s, c_s, s_s) = refs
    l = pl.program_id(1)
    halo = SUBLANES
    width = 3 * W_DELTA

    @pl.when(l == 0)
    def _():
        if has_state:
            u_s[0:halo, :] = jnp.zeros((halo, width), F32)
            u_s[halo - (CONV_W - 1):halo, :] = prev_ref[0]
            s_s[...] = s0_ref[0]
        else:
            u_s[0:halo, :] = jnp.zeros((halo, width), F32)
            s_s[...] = jnp.zeros(s_s.shape, F32)

    @pl.when(l > 0)
    def _():
        u_s[0:halo, :] = u_s[lt:lt + halo, :]

    u_s[halo:halo + lt, 0:W_DELTA] = zq_ref[0]
    u_s[halo:halo + lt, W_DELTA:2 * W_DELTA] = zk_ref[0]
    u_s[halo:halo + lt, 2 * W_DELTA:width] = zv_ref[0]

    base = halo - (CONV_W - 1)
    acc = cw_ref[0:1, :] * u_s[base:base + lt, :]
    for i in range(1, CONV_W):
        acc = acc + cw_ref[i:i + 1, :] * u_s[base + i:base + i + lt, :]
    c_s[...] = acc * _sigmoid(acc)

    neg_a = -jnp.exp(a_ref[...])

    def chunk_body(ci, carry):
        r0 = pl.multiple_of(ci * chunk, chunk)
        zs = zs_ref[0, pl.ds(r0, chunk), :]
        beta_all = _sigmoid(zs)
        sp_in = zs + dt_ref[...]
        g_all = neg_a * (jnp.maximum(sp_in, 0.0) + jnp.log1p(jnp.exp(-jnp.abs(sp_in))))
        for h in range(H_DELTA):
            lo = h * HEAD_DIM
            cq = c_s[pl.ds(r0, chunk), lo:lo + HEAD_DIM]
            ck = c_s[pl.ds(r0, chunk), W_DELTA + lo:W_DELTA + lo + HEAD_DIM]
            cv = c_s[pl.ds(r0, chunk), 2 * W_DELTA + lo:2 * W_DELTA + lo + HEAD_DIM]
            beta_b = jnp.broadcast_to(beta_all[:, h:h + 1], (chunk, HEAD_DIM))
            g_b = jnp.broadcast_to(g_all[:, H_DELTA + h:H_DELTA + h + 1], (chunk, HEAD_DIM))
            o, s_new = _delta_chunk(cq, ck, cv, beta_b, g_b, s_s[h])
            s_s[h] = s_new
            zg = zg_ref[0, pl.ds(r0, chunk), lo:lo + HEAD_DIM]
            o = _rms(o, og_ref[...]) * (zg * _sigmoid(zg))
            o_ref[0, pl.ds(r0, chunk), lo:lo + HEAD_DIM] = o.astype(o_ref.dtype)
        return carry

    lax.fori_loop(0, lt // chunk, chunk_body, 0)

    @pl.when(l == pl.num_programs(1) - 1)
    def _():
        sout_ref[0] = s_s[...]
        cout_ref[0] = u_s[halo + lt - (CONV_W - 1):halo + lt, :]


def _delta_mixer(z3, zs3, conv_w, a_vec, dt_vec, o_norm_g, lt, chunk, conv_prev=None, s0=None):
    B, L, _ = z3.shape
    has_state = conv_prev is not None
    width = 3 * W_DELTA
    col_blk = lambda c: pl.BlockSpec((1, lt, W_DELTA), lambda b, l, c=c: (b, l, c))
    full = lambda shape: pl.BlockSpec(shape, lambda b, l: (0,) * len(shape))
    in_specs = [col_blk(0), col_blk(1), col_blk(2), col_blk(3),
                pl.BlockSpec((1, lt, LANES), lambda b, l: (b, l, 0)),
                full((CONV_W, width)), full((1, LANES)), full((1, LANES)), full((1, HEAD_DIM))]
    args = [z3, z3, z3, z3, zs3, conv_w, a_vec, dt_vec, o_norm_g]
    if has_state:
        in_specs += [pl.BlockSpec((1, CONV_W - 1, width), lambda b, l: (b, 0, 0)),
                     pl.BlockSpec((1, H_DELTA, HEAD_DIM, HEAD_DIM), lambda b, l: (b, 0, 0, 0))]
        args += [conv_prev, s0]
    blocks = (4 * _nbytes((lt, W_DELTA), F32) + _nbytes((lt, LANES), F32) + _nbytes((CONV_W, width), F32)
              + _nbytes((lt, W_DELTA), BF16) + 2 * _nbytes((H_DELTA, HEAD_DIM, HEAD_DIM), F32)
              + 2 * _nbytes((SUBLANES, width), F32))
    scratch = [pltpu.VMEM((SUBLANES + lt, width), F32), pltpu.VMEM((lt, width), F32),
               pltpu.VMEM((H_DELTA, HEAD_DIM, HEAD_DIM), F32)]
    scratch_bytes = (_nbytes((SUBLANES + lt, width), F32) + _nbytes((lt, width), F32)
                     + _nbytes((H_DELTA, HEAD_DIM, HEAD_DIM), F32))
    return pl.pallas_call(
        functools.partial(_delta_kernel, lt=lt, chunk=chunk, has_state=has_state),
        grid=(B, L // lt),
        in_specs=in_specs,
        out_specs=[
            pl.BlockSpec((1, lt, W_DELTA), lambda b, l: (b, l, 0)),
            pl.BlockSpec((1, H_DELTA, HEAD_DIM, HEAD_DIM), lambda b, l: (b, 0, 0, 0)),
            pl.BlockSpec((1, CONV_W - 1, width), lambda b, l: (b, 0, 0)),
        ],
        out_shape=[jax.ShapeDtypeStruct((B, L, W_DELTA), BF16),
                   jax.ShapeDtypeStruct((B, H_DELTA, HEAD_DIM, HEAD_DIM), F32),
                   jax.ShapeDtypeStruct((B, CONV_W - 1, width), F32)],
        scratch_shapes=scratch,
        compiler_params=pltpu.CompilerParams(
            dimension_semantics=("parallel", "arbitrary"),
            vmem_limit_bytes=_vmem_limit(blocks, scratch_bytes)),
        name="delta_mixer",
    )(*args)


def _split_bf16(x):
    hi = x.astype(BF16)
    return hi, (x - hi.astype(F32)).astype(BF16)


def _dot3(a, b, dot=jnp.dot):
    ah, al = _split_bf16(a)
    bh, bl = _split_bf16(b)
    kw = dict(preferred_element_type=F32)
    if a.shape[-1] == LANES:
        b_axis = 0 if dot is jnp.dot else 1
        return dot(jnp.concatenate([ah, ah, al], 1), jnp.concatenate([bh, bl, bh], b_axis), **kw)
    return dot(ah, bh, **kw) + dot(ah, bl, **kw) + dot(al, bh, **kw)


def _dot_sel(sel, x):
    x1 = x.astype(BF16)
    r1 = x - x1.astype(F32)
    x2 = r1.astype(BF16)
    x3 = (r1 - x2.astype(F32)).astype(BF16)
    kw = dict(preferred_element_type=F32)
    return jnp.dot(sel, x1, **kw) + jnp.dot(sel, x2, **kw) + jnp.dot(sel, x3, **kw)


def _beta_g(zs, neg_a, dt):
    sp_in = zs + dt
    return _sigmoid(zs), neg_a * (jnp.maximum(sp_in, 0.0) + jnp.log1p(jnp.exp(-jnp.abs(sp_in))))


def _delta_prompt_kernel(zq_ref, zk_ref, zv_ref, zg_ref, zs_ref, cw_ref, a_ref, dt_ref, og_ref,
                         o_ref, sout_ref, cout_ref,
                         u_s, c_s, s_s, uu_s, ww_s, qe_s, kd_s, qk_s, egl_s, *, lt):
    C = CHUNK
    P2 = 2 * C
    n_pairs = H_DELTA // 2
    l = pl.program_id(1)
    halo = SUBLANES
    width = 3 * W_DELTA

    @pl.when(l == 0)
    def _():
        u_s[0:halo, :] = jnp.zeros((halo, width), F32)
        s_s[...] = jnp.zeros(s_s.shape, F32)

    @pl.when(l > 0)
    def _():
        u_s[0:halo, :] = u_s[lt:lt + halo, :]

    u_s[halo:halo + lt, 0:W_DELTA] = zq_ref[0]
    u_s[halo:halo + lt, W_DELTA:2 * W_DELTA] = zk_ref[0]
    u_s[halo:halo + lt, 2 * W_DELTA:width] = zv_ref[0]

    base = halo - (CONV_W - 1)
    acc = cw_ref[0:1, :] * u_s[base:base + lt, :]
    for i in range(1, CONV_W):
        acc = acc + cw_ref[i:i + 1, :] * u_s[base + i:base + i + lt, :]
    c_s[...] = acc * _sigmoid(acc)

    neg_a = -jnp.exp(a_ref[...])
    row = lax.broadcasted_iota(jnp.int32, (P2, P2), 0)
    col = lax.broadcasted_iota(jnp.int32, (P2, P2), 1)
    same_head = (row & C) == (col & C)
    m_strict = jnp.where(same_head & (row > col), 1.0, 0.0)
    m_incl = jnp.where(same_head & (row >= col), 1.0, 0.0)
    eye = jnp.where(row == col, 1.0, 0.0)
    tri = jnp.where(lax.broadcasted_iota(jnp.int32, (C, C), 0) >= lax.broadcasted_iota(jnp.int32, (C, C), 1),
                    1.0, 0.0).astype(BF16)
    lane = lax.broadcasted_iota(jnp.int32, (1, P2), 1)
    head_lanes = lambda h: slice(h * HEAD_DIM, (h + 1) * HEAD_DIM)
    pairs = range(n_pairs)

    def phase_a(ci, carry):
        rows = pl.ds(pl.multiple_of(ci * C, C), C)
        beta_all, g_all = _beta_g(zs_ref[0, rows, :], neg_a, dt_ref[...])
        gam_all = _dot_sel(tri, g_all)
        gam_t = jnp.concatenate([gam_all, gam_all], 0).T

        k2, q2, rhs2, beta2, gcol, grow, qe, kdec = [], [], [], [], [], [], [], []
        for p in pairs:
            ks, qs, vs, bs, gs = [], [], [], [], []
            for h in (2 * p, 2 * p + 1):
                cq = c_s[rows, head_lanes(h)]
                ck = c_s[rows, W_DELTA + h * HEAD_DIM:W_DELTA + (h + 1) * HEAD_DIM]
                cv = c_s[rows, 2 * W_DELTA + h * HEAD_DIM:2 * W_DELTA + (h + 1) * HEAD_DIM]
                q = cq * lax.rsqrt(jnp.sum(cq * cq, -1, keepdims=True) + EPS) * ATTN_SCALE
                k = ck * lax.rsqrt(jnp.sum(ck * ck, -1, keepdims=True) + EPS)
                gam_b = jnp.broadcast_to(gam_all[:, H_DELTA + h:H_DELTA + h + 1], (C, HEAD_DIM))
                gam_last = gam_b[C - 1:C, :]
                kd_s[rows, head_lanes(h)] = k * jnp.exp(gam_last - gam_b)
                egl_s[pl.ds(ci * H_DELTA + h, 1), :] = jnp.exp(gam_last)
                ks.append(k)
                qs.append(q)
                vs.append(cv)
                bs.append(jnp.broadcast_to(beta_all[:, h:h + 1], (C, HEAD_DIM)))
                gs.append(gam_b)
            k2.append(jnp.concatenate(ks, 0))
            q2.append(jnp.concatenate(qs, 0))
            beta2.append(jnp.concatenate(bs, 0))
            gcol.append(jnp.concatenate(gs, 0))
            grow.append(jnp.where(lane < C, gam_t[H_DELTA + 2 * p:H_DELTA + 2 * p + 1, :],
                                  gam_t[H_DELTA + 2 * p + 1:H_DELTA + 2 * p + 2, :]))
            eg = jnp.exp(gcol[p])
            qe.append(eg * q2[p])
            rhs2.append(jnp.concatenate([beta2[p] * jnp.concatenate(vs, 0), beta2[p] * eg * k2[p]], 1))

        kq = [_dot3(jnp.concatenate([k2[p], q2[p]], 0), k2[p], dot=_dot_nt) for p in pairs]
        dec = [jnp.exp(jnp.minimum(gcol[p] - grow[p], 0.0)) for p in pairs]
        for p in pairs:
            qk_s[ci * n_pairs + p] = kq[p][P2:] * (dec[p] * m_incl)
        xs = [-(beta2[p] * kq[p][:P2] * (dec[p] * m_strict)) for p in pairs]
        invs = [eye + x for x in xs]
        xs = [_dot3(x, x) for x in xs]
        for _ in range(int(math.log2(C)) - 2):
            ys = [_dot3(jnp.concatenate([inv, x], 0), x) for inv, x in zip(invs, xs)]
            invs = [inv + y[:P2] for inv, y in zip(invs, ys)]
            xs = [y[P2:] for y in ys]
        invs = [inv + _dot3(inv, x) for inv, x in zip(invs, xs)]
        uw = [_dot3(invs[p], rhs2[p]) for p in pairs]
        for p in pairs:
            for hh in range(2):
                h = 2 * p + hh
                uu_s[rows, head_lanes(h)] = uw[p][hh * C:(hh + 1) * C, :HEAD_DIM]
                ww_s[rows, head_lanes(h)] = uw[p][hh * C:(hh + 1) * C, HEAD_DIM:]
                qe_s[rows, head_lanes(h)] = qe[p][hh * C:(hh + 1) * C, :]
        return carry

    lax.fori_loop(0, lt // C, phase_a, 0)

    def phase_b(ci, carry):
        rows = pl.ds(pl.multiple_of(ci * C, C), C)
        heads = range(H_DELTA)
        wq_s = [_dot3(jnp.concatenate([ww_s[rows, head_lanes(h)], qe_s[rows, head_lanes(h)]], 0), s_s[h], dot=_dot_nt)
                for h in heads]
        delta = [uu_s[rows, head_lanes(h)] - wq_s[h][:C] for h in heads]
        od = [_dot3(qk_s[ci * n_pairs + p], jnp.concatenate([delta[2 * p], delta[2 * p + 1]], 0)) for p in pairs]
        for h in heads:
            o = wq_s[h][C:] + od[h // 2][(h % 2) * C:(h % 2 + 1) * C]
            zg = zg_ref[0, rows, head_lanes(h)]
            o_ref[0, rows, head_lanes(h)] = (_rms(o, og_ref[...]) * (zg * _sigmoid(zg))).astype(o_ref.dtype)
        for h in heads:
            s_s[h] = egl_s[pl.ds(ci * H_DELTA + h, 1), :] * s_s[h] + _dot3(delta[h].T, kd_s[rows, head_lanes(h)])
        return carry

    lax.fori_loop(0, lt // C, phase_b, 0)

    @pl.when(l == pl.num_programs(1) - 1)
    def _():
        sout_ref[0] = s_s[...]
        cout_ref[0] = u_s[halo + lt - (CONV_W - 1):halo + lt, :]


def _delta_prompt(z3, zs3, conv_w, a_vec, dt_vec, o_norm_g, lt):
    B, L, _ = z3.shape
    width = 3 * W_DELTA
    n_chunks = lt // CHUNK
    col_blk = lambda c: pl.BlockSpec((1, lt, W_DELTA), lambda b, l, c=c: (b, l, c))
    full = lambda shape: pl.BlockSpec(shape, lambda b, l: (0,) * len(shape))
    blocks = (4 * _nbytes((lt, W_DELTA), F32) + _nbytes((lt, LANES), F32) + _nbytes((CONV_W, width), F32)
              + _nbytes((lt, W_DELTA), BF16) + _nbytes((H_DELTA, HEAD_DIM, HEAD_DIM), F32)
              + _nbytes((SUBLANES, width), F32))
    scratch_dims = [(SUBLANES + lt, width), (lt, width), (H_DELTA, HEAD_DIM, HEAD_DIM),
                    (lt, W_DELTA), (lt, W_DELTA), (lt, W_DELTA), (lt, W_DELTA),
                    (n_chunks * H_DELTA // 2, 2 * CHUNK, 2 * CHUNK), (n_chunks * H_DELTA, HEAD_DIM)]
    return pl.pallas_call(
        functools.partial(_delta_prompt_kernel, lt=lt),
        grid=(B, L // lt),
        in_specs=[col_blk(0), col_blk(1), col_blk(2), col_blk(3),
                  pl.BlockSpec((1, lt, LANES), lambda b, l: (b, l, 0)),
                  full((CONV_W, width)), full((1, LANES)), full((1, LANES)), full((1, HEAD_DIM))],
        out_specs=[
            pl.BlockSpec((1, lt, W_DELTA), lambda b, l: (b, l, 0)),
            pl.BlockSpec((1, H_DELTA, HEAD_DIM, HEAD_DIM), lambda b, l: (b, 0, 0, 0)),
            pl.BlockSpec((1, CONV_W - 1, width), lambda b, l: (b, 0, 0)),
        ],
        out_shape=[jax.ShapeDtypeStruct((B, L, W_DELTA), BF16),
                   jax.ShapeDtypeStruct((B, H_DELTA, HEAD_DIM, HEAD_DIM), F32),
                   jax.ShapeDtypeStruct((B, CONV_W - 1, width), F32)],
        scratch_shapes=[pltpu.VMEM(d, F32) for d in scratch_dims],
        compiler_params=pltpu.CompilerParams(
            dimension_semantics=("parallel", "arbitrary"),
            vmem_limit_bytes=_vmem_limit(blocks, sum(_nbytes(d, F32) for d in scratch_dims))),
        name="delta_prompt",
    )(z3, z3, z3, z3, zs3, conv_w, a_vec, dt_vec, o_norm_g)


def _norm_rot(x, g, cos2, sin2):
    xn = _rms(x, g)
    return xn * cos2 + pltpu.roll(xn, HEAD_DIM // 2, 1) * sin2


def _softmax_parts(s, mask, vb):
    s = jnp.where(mask, s, -jnp.inf)
    m = jnp.max(s, -1, keepdims=True)
    p = jnp.exp(s - m)
    l = jnp.sum(p, -1, keepdims=True)
    o = jnp.dot(p.astype(BF16), vb.astype(BF16), preferred_element_type=F32)
    return m, l, o


def _merge(parts):
    m_all = functools.reduce(jnp.maximum, [p[0] for p in parts])
    ws = [jnp.exp(p[0] - m_all) for p in parts]
    num = sum(w * p[2] for w, p in zip(ws, parts))
    den = sum(w * p[1] for w, p in zip(ws, parts))
    return num / den


ATT_QSUB = 64


def _band_mask(nq, nk, shift, window):
    dist = (lax.broadcasted_iota(jnp.int32, (nq, nk), 0) + shift) - lax.broadcasted_iota(jnp.int32, (nq, nk), 1)
    return (dist >= 0) & (dist <= window)
def _attn_prompt_kernel(aq_ref, ak_ref, av_ref, cos_ref, sin_ref, qg_ref, kg_ref,
                        ob_ref, kout_ref, vout_ref, q_s, k_s, qd_s, kd_s, vd_s, o_s, m_s, l_s):
    L = q_s.shape[0]
    blk = ATT_BLOCK
    rows = 256

    def prep(i, carry):
        r = pl.multiple_of(i * rows, rows)
        sl = pl.ds(r, rows)
        cos2, sin2 = cos_ref[sl, :], sin_ref[sl, :]
        q = _norm_rot(aq_ref[0, sl, :], qg_ref[...], cos2, sin2) * ATTN_SCALE
        k = _norm_rot(ak_ref[0, sl, :], kg_ref[...], cos2, sin2)
        v = av_ref[0, sl, :]
        q_s[sl, :] = q
        k_s[sl, :] = k
        kout_ref[0, sl, :] = k
        vout_ref[0, sl, :] = v
        qd_s[0, sl, :] = q.astype(BF16)
        kd_s[0, sl, :] = k.astype(BF16)
        vd_s[0, sl, :] = v.astype(BF16)
        return carry

    lax.fori_loop(0, L // rows, prep, 0, unroll=4)

    for branch, (window, dil) in enumerate(DIL_PAIRS):
        assert window // dil == blk
        n = L // dil
        for r in range(dil if dil > 1 else 0):
            src = pl.ds(r, n, stride=dil)
            dst = pl.ds(r * n, n)
            qd_s[branch, dst, :] = q_s[src, :].astype(BF16)
            kd_s[branch, dst, :] = k_s[src, :].astype(BF16)
            vd_s[branch, dst, :] = av_ref.at[0][src, :].astype(BF16)

    qs = ATT_QSUB
    band_mask = _band_mask(qs, qs + blk, blk, blk)
    first_masks = [_band_mask(qs, qo + qs, qo, blk) for qo in range(0, blk, qs)]

    def tiles(dil, r, i, first):
        n = L // dil
        out = []
        for qo in range(0, blk, qs):
            q0 = r * n + i * blk + qo
            k0, nk, mask = (q0 - qo, qo + qs, first_masks[qo // qs]) if first else (q0 - blk, qs + blk, band_mask)
            if not isinstance(q0, int):
                q0, k0 = pl.multiple_of(q0, qs), pl.multiple_of(k0, qs)
            out.append((q0, k0, nk, mask, pl.ds(r + dil * (blk * i + qo), qs, stride=dil if dil > 1 else None)))
        return out

    def run(branch, blocks):
        ts = [t for blk_args in blocks for t in tiles(*blk_args)]
        s = [_dot_nt(qd_s[branch, pl.ds(q0, qs), :], kd_s[branch, pl.ds(k0, nk), :], preferred_element_type=F32)
             for q0, k0, nk, _, _ in ts]
        s = [jnp.where(t[3], x, -jnp.inf) for t, x in zip(ts, s)]
        m = [jnp.max(x, -1, keepdims=True) for x in s]
        p = [jnp.exp(x - mx) for x, mx in zip(s, m)]
        l = [jnp.sum(x, -1, keepdims=True) for x in p]
        o = [jnp.dot(x.astype(BF16), vd_s[branch, pl.ds(t[1], t[2]), :], preferred_element_type=F32)
             for t, x in zip(ts, p)]
        for t, mx, lx, ox in zip(ts, m, l, o):
            o_s.at[branch][t[4], :] = ox
            m_s.at[branch][t[4], :] = jnp.broadcast_to(mx, (qs, HEAD_DIM))
            l_s.at[branch][t[4], :] = jnp.broadcast_to(lx, (qs, HEAD_DIM))

    group = 5
    for branch, (window, dil) in enumerate(DIL_PAIRS):
        nb = L // dil // blk
        if dil == 1:
            assert (nb - 1) % group == 0
            run(branch, [(dil, 0, 0, True)])

            def later(g, carry, branch=branch, dil=dil):
                run(branch, [(dil, 0, 1 + g * group + u, False) for u in range(group)])
                return carry

            lax.fori_loop(0, (nb - 1) // group, later, 0)
        else:
            per_body = max(1, 8 // nb)

            def residues(g, carry, branch=branch, dil=dil, nb=nb, per_body=per_body):
                run(branch, [(dil, g * per_body + u, i, i == 0) for u in range(per_body) for i in range(nb)])
                return carry

            lax.fori_loop(0, dil // per_body, residues, 0)

    def merge(i, carry):
        sl = pl.ds(pl.multiple_of(i * rows, rows), rows)
        parts = [(m_s[b, sl, :], l_s[b, sl, :], o_s[b, sl, :]) for b in range(len(DIL_PAIRS))]
        ob_ref[0, sl, :] = _merge(parts).astype(ob_ref.dtype)
        return carry

    lax.fori_loop(0, L // rows, merge, 0)


def _attn_prompt(z3, cos2, sin2, q_norm_g, k_norm_g):
    B, L, _ = z3.shape
    nbr = len(DIL_PAIRS)
    head_blk = lambda c0: pl.BlockSpec((1, L, HEAD_DIM), lambda b, h, c0=c0: (b, 0, c0 + h))
    out_blk = pl.BlockSpec((1, L, HEAD_DIM), lambda b, h: (b, 0, h))
    tab = pl.BlockSpec((L, HEAD_DIM), lambda b, h: (0, 0))
    vec = pl.BlockSpec((1, HEAD_DIM), lambda b, h: (0, 0))
    blocks = 5 * _nbytes((L, HEAD_DIM), F32) + 2 * _nbytes((L, HEAD_DIM), F32) + _nbytes((L, HEAD_DIM), BF16)
    scratch_bytes = (2 + 3 * nbr) * _nbytes((L, HEAD_DIM), F32) + 3 * nbr * _nbytes((L, HEAD_DIM), BF16)
    return pl.pallas_call(
        _attn_prompt_kernel,
        grid=(B, H_ATTN),
        in_specs=[head_blk(COL_AQ), head_blk(COL_AK), head_blk(COL_AV), tab, tab, vec, vec],
        out_specs=[out_blk, out_blk, out_blk],
        out_shape=[jax.ShapeDtypeStruct((B, L, W_ATTN), BF16),
                   jax.ShapeDtypeStruct((B, L, W_ATTN), F32),
                   jax.ShapeDtypeStruct((B, L, W_ATTN), F32)],
        scratch_shapes=[pltpu.VMEM((L, HEAD_DIM), F32), pltpu.VMEM((L, HEAD_DIM), F32),
                        pltpu.VMEM((nbr, L, HEAD_DIM), BF16), pltpu.VMEM((nbr, L, HEAD_DIM), BF16),
                        pltpu.VMEM((nbr, L, HEAD_DIM), BF16),
                        pltpu.VMEM((nbr, L, HEAD_DIM), F32), pltpu.VMEM((nbr, L, HEAD_DIM), F32),
                        pltpu.VMEM((nbr, L, HEAD_DIM), F32)],
        compiler_params=pltpu.CompilerParams(
            dimension_semantics=("parallel", "parallel"),
            vmem_limit_bytes=_vmem_limit(blocks, scratch_bytes)),
        name="attn_prompt",
    )(z3, z3, z3, cos2, sin2, q_norm_g, k_norm_g)


def _attn_sample_kernel(aq_ref, ak_ref, av_ref, ck_ref, cv_ref, ckh_ref, cvh_ref, cos_ref, sin_ref, qg_ref, kg_ref,
                        ob_ref, wk_ref, wv_ref, q_s, kn_s, m_s, l_s, acc_s):
    T = aq_ref.shape[1]
    H = H_ATTN
    rows = ck_ref.shape[1]
    P = rows // H
    j = pl.program_id(1)
    nj = pl.num_programs(1)
    wbuf = P * nj
    shift = T * H
    nbr = len(DIL_PAIRS)
    head_lanes = lambda h: slice(h * HEAD_DIM, (h + 1) * HEAD_DIM)

    @pl.when(j == 0)
    def _():
        cos2, sin2 = cos_ref[...], sin_ref[...]
        for h in range(H):
            q_s[h] = _norm_rot(aq_ref[0, :, head_lanes(h)], qg_ref[...], cos2, sin2)
            kn_s[h] = _norm_rot(ak_ref[0, :, head_lanes(h)], kg_ref[...], cos2, sin2)
        m_s[...] = jnp.full(m_s.shape, -jnp.inf, F32)
        l_s[...] = jnp.zeros(l_s.shape, F32)
        acc_s[...] = jnp.zeros(acc_s.shape, F32)

    def update(keys, vals, masks):
        units = [(br, h) for h in range(H) for br in range(nbr)]
        s = [_dot_nt(q_s[h].astype(BF16), keys[h].astype(BF16), preferred_element_type=F32) * ATTN_SCALE
             for h in range(H)]
        vb = [v.astype(BF16) for v in vals]
        sm = [jnp.where(masks[br], s[h], -jnp.inf) for br, h in units]
        m_old = [m_s[br, h] for br, h in units]
        m_new = [jnp.maximum(mo, jnp.max(x, -1, keepdims=True)) for mo, x in zip(m_old, sm)]
        m_safe = [jnp.where(mn == -jnp.inf, 0.0, mn) for mn in m_new]
        p = [jnp.exp(x - ms) for x, ms in zip(sm, m_safe)]
        alpha = [jnp.exp(mo - ms) for mo, ms in zip(m_old, m_safe)]
        pv = [jnp.dot(x.astype(BF16), vb[h], preferred_element_type=F32) for x, (br, h) in zip(p, units)]
        for u, (br, h) in enumerate(units):
            l_s[br, h] = alpha[u] * l_s[br, h] + jnp.sum(p[u], -1, keepdims=True)
            acc_s[br, h] = alpha[u] * acc_s[br, h] + pv[u]
            m_s[br, h] = m_new[u]

    t_c = lax.broadcasted_iota(jnp.int32, (T, P), 0)
    dist_c = wbuf + t_c - (j * P + lax.broadcasted_iota(jnp.int32, (T, P), 1))
    update([ck_ref.at[0][pl.ds(h, P, stride=H), :] for h in range(H)],
           [cv_ref.at[0][pl.ds(h, P, stride=H), :] for h in range(H)],
           [((dist_c & (dil - 1)) == 0) & (dist_c <= window) for window, dil in DIL_PAIRS])

    wk_ref[0, 0:rows - shift, :] = ck_ref[0, shift:rows, :]
    wv_ref[0, 0:rows - shift, :] = cv_ref[0, shift:rows, :]

    @pl.when(j < nj - 1)
    def _():
        wk_ref[0, rows - shift:rows, :] = ckh_ref[0]
        wv_ref[0, rows - shift:rows, :] = cvh_ref[0]

    @pl.when(j == nj - 1)
    def _():
        t_n = lax.broadcasted_iota(jnp.int32, (T, T), 0)
        dist_n = t_n - lax.broadcasted_iota(jnp.int32, (T, T), 1)
        update([kn_s[h] for h in range(H)], [av_ref[0, :, head_lanes(h)] for h in range(H)],
               [(dist_n >= 0) & ((dist_n & (dil - 1)) == 0) & (dist_n <= window) for window, dil in DIL_PAIRS])
        for h in range(H):
            parts = [(m_s[br, h], l_s[br, h], acc_s[br, h]) for br in range(nbr)]
            ob_ref[0, :, head_lanes(h)] = _merge(parts).astype(ob_ref.dtype)
            wk_ref.at[0][pl.ds(rows - shift + h, T, stride=H), :] = kn_s[h]
            wv_ref.at[0][pl.ds(rows - shift + h, T, stride=H), :] = av_ref[0, :, head_lanes(h)]


def _attn_sample(z3, cache_k, cache_v, cos2, sin2, q_norm_g, k_norm_g, pos_chunk=1024):
    B, T, _ = z3.shape
    rows_all = cache_k.shape[1]
    rows = pos_chunk * H_ATTN
    nj = rows_all // rows
    shift = T * H_ATTN
    per_chunk = rows // shift
    zblk = lambda c0: pl.BlockSpec((1, T, W_ATTN), lambda b, j, c0=c0: (b, 0, c0 // H_ATTN))
    cblk = pl.BlockSpec((1, rows, HEAD_DIM), lambda b, j: (b, j, 0))
    halo = pl.BlockSpec((1, shift, HEAD_DIM), lambda b, j: (b, jnp.minimum((j + 1) * per_chunk, nj * per_chunk - 1), 0))
    tab = pl.BlockSpec((T, HEAD_DIM), lambda b, j: (0, 0))
    vec = pl.BlockSpec((1, HEAD_DIM), lambda b, j: (0, 0))
    nbr = len(DIL_PAIRS)
    blocks = 4 * _nbytes((rows, HEAD_DIM), F32) + 2 * _nbytes((shift, HEAD_DIM), F32) + 4 * _nbytes((T, W_ATTN), F32)
    scratch_dims = [(H_ATTN, T, HEAD_DIM), (H_ATTN, T, HEAD_DIM), (nbr, H_ATTN, T, 1), (nbr, H_ATTN, T, 1),
                    (nbr, H_ATTN, T, HEAD_DIM)]
    return pl.pallas_call(
        _attn_sample_kernel,
        grid=(B, nj),
        in_specs=[zblk(COL_AQ), zblk(COL_AK), zblk(COL_AV), cblk, cblk, halo, halo, tab, tab, vec, vec],
        out_specs=[pl.BlockSpec((1, T, W_ATTN), lambda b, j: (b, 0, 0)), cblk, cblk],
        out_shape=[jax.ShapeDtypeStruct((B, T, W_ATTN), BF16),
                   jax.ShapeDtypeStruct((B, rows_all, HEAD_DIM), F32),
                   jax.ShapeDtypeStruct((B, rows_all, HEAD_DIM), F32)],
        scratch_shapes=[pltpu.VMEM(d, F32) for d in scratch_dims],
        compiler_params=pltpu.CompilerParams(
            dimension_semantics=("parallel", "arbitrary"),
            vmem_limit_bytes=_vmem_limit(blocks, sum(_nbytes(d, F32) for d in scratch_dims))),
        name="attn_sample",
    )(z3, z3, z3, cache_k, cache_v, cache_k, cache_v, cos2, sin2, q_norm_g, k_norm_g)


def _route(logits):
    lane = lax.broadcasted_iota(jnp.int32, logits.shape, 1).astype(F32)
    first_max = lambda p, top, ok: jnp.min(jnp.where(ok & (p == top), lane, float(LANES)), -1, keepdims=True)
    is_group = lane < N_GROUPS
    gl = jnp.where(is_group, logits, -jnp.inf)
    pg = jnp.exp(gl - jnp.max(gl, -1, keepdims=True))
    pg = pg / jnp.sum(pg, -1, keepdims=True)
    pg_top = jnp.max(pg, -1, keepdims=True)
    g_top = first_max(pg, pg_top, is_group)
    lo = N_GROUPS + g_top * EXPERTS_PER_GROUP
    in_group = (lane >= lo) & (lane < lo + EXPERTS_PER_GROUP)
    el = jnp.where(in_group, logits, -jnp.inf)
    pe = jnp.exp(el - jnp.max(el, -1, keepdims=True))
    pe = pe / jnp.sum(pe, -1, keepdims=True)
    p1 = jnp.max(pe, -1, keepdims=True)
    l1 = first_max(pe, p1, in_group)
    rest = in_group & (lane != l1)
    p2 = jnp.max(jnp.where(rest, pe, -1.0), -1, keepdims=True)
    l2 = first_max(pe, p2, rest)
    denom = p1 + p2
    gates = jnp.where(lane == 0.0, pg_top * p1 / denom, jnp.where(lane == 1.0, pg_top * p2 / denom, 0.0))
    experts = jnp.where(lane == 0.0, l1 - N_GROUPS, jnp.where(lane == 1.0, l2 - N_GROUPS, 0.0))
    return gates, experts.astype(jnp.int32)


def _out_proj_kernel(oa_ref, ob_ref, x_ref, wa_ref, wb_ref, g_ref, wrh_ref, wrl_ref, br_ref, h_ref, gate_ref, exp_ref):
    h = (x_ref[...] + jnp.dot(oa_ref[...], wa_ref[...], preferred_element_type=F32)
         + jnp.dot(ob_ref[...], wb_ref[...], preferred_element_type=F32))
    h_ref[...] = h
    hn_hi, hn_lo = _split_bf16(_rms(h, g_ref[...]))
    w_hi, w_lo = wrh_ref[...], wrl_ref[...]
    logits = (jnp.dot(hn_hi, w_hi, preferred_element_type=F32) + jnp.dot(hn_hi, w_lo, preferred_element_type=F32)
              + jnp.dot(hn_lo, w_hi, preferred_element_type=F32)) + br_ref[...]
    gate_ref[...], exp_ref[...] = _route(logits)


def _out_proj(o_a, o_b, x2d, w_a, w_b, g, w_rt_hi, w_rt_lo, b_rt, tm):
    T = x2d.shape[0]
    row = lambda w: pl.BlockSpec((tm, w), lambda i: (i, 0))
    full = lambda shape: pl.BlockSpec(shape, lambda i: (0, 0))
    blocks = (2 * _nbytes((tm, W_DELTA), BF16) + 2 * _nbytes((tm, D_MODEL), F32) + 2 * _nbytes((W_DELTA, D_MODEL), BF16)
              + 2 * _nbytes((D_MODEL, LANES), BF16) + 2 * _nbytes((tm, LANES), F32))
    return pl.pallas_call(
        _out_proj_kernel,
        grid=(T // tm,),
        in_specs=[row(W_DELTA), row(W_ATTN), row(D_MODEL), full((W_DELTA, D_MODEL)), full((W_ATTN, D_MODEL)),
                  full((1, D_MODEL)), full((D_MODEL, LANES)), full((D_MODEL, LANES)), full((1, LANES))],
        out_specs=[row(D_MODEL), row(LANES), row(LANES)],
        out_shape=[jax.ShapeDtypeStruct((T, D_MODEL), F32), jax.ShapeDtypeStruct((T, LANES), F32),
                   jax.ShapeDtypeStruct((T, LANES), jnp.int32)],
        compiler_params=pltpu.CompilerParams(
            dimension_semantics=("parallel",), vmem_limit_bytes=_vmem_limit(blocks)),
        name="out_proj",
    )(o_a, o_b, x2d, w_a, w_b, g, w_rt_hi, w_rt_lo, b_rt)


def _row_gather(idx_ref, n, src_hbm, dst, sem):
    for r in range(n):
        pltpu.make_async_copy(src_hbm.at[pl.ds(idx_ref[0, 0, r], 1)], dst.at[pl.ds(r, 1)], sem).start()


def _expert_kernel(be_ref, nv_ref, tok_ref, tok_next_ref, h_hbm, g_ref, wg_ref, wu_ref, wd_ref, y_ref, xbuf, sem):
    del be_ref
    tm = xbuf.shape[1]
    i = pl.program_id(0)
    n_valid = nv_ref[0]
    slot = lax.rem(i, 2)

    @pl.when((i == 0) & (n_valid > 0))
    def _():
        _row_gather(tok_ref, tm, h_hbm, xbuf.at[0], sem.at[0])

    @pl.when(i + 1 < n_valid)
    def _():
        _row_gather(tok_next_ref, tm, h_hbm, xbuf.at[1 - slot], sem.at[1 - slot])

    @pl.when(i < n_valid)
    def _():
        pltpu.make_async_copy(h_hbm.at[pl.ds(0, tm)], xbuf.at[slot], sem.at[slot]).wait()
        x = _rms(xbuf[slot], g_ref[...]).astype(BF16)
        a = jnp.dot(x, wg_ref[0], preferred_element_type=F32)
        b = jnp.dot(x, wu_ref[0], preferred_element_type=F32)
        hmid = (a * _sigmoid(a)) * b
        y_ref[...] = jnp.dot(hmid.astype(BF16), wd_ref[0], preferred_element_type=F32)

    @pl.when(i >= n_valid)
    def _():
        y_ref[...] = jnp.zeros(y_ref.shape, F32)


def _experts(block_e, n_valid, rows_tok, h, g, w_g, w_u, w_d, tm):
    nb = rows_tok.shape[0]
    last = lambda i, nv: jnp.minimum(i, jnp.maximum(nv[0] - 1, 0))
    smem_blk = lambda f: pl.BlockSpec((1, 1, tm), f, memory_space=pltpu.SMEM)
    blocks = 3 * _nbytes((D_MODEL, EXPERT_FF), BF16) + _nbytes((tm, D_MODEL), F32)
    scratch_bytes = 2 * _nbytes((tm, D_MODEL), F32)
    return pl.pallas_call(
        _expert_kernel,
        grid_spec=pltpu.PrefetchScalarGridSpec(
            num_scalar_prefetch=2,
            grid=(nb,),
            in_specs=[
                smem_blk(lambda i, be, nv: (last(i, nv), 0, 0)),
                smem_blk(lambda i, be, nv: (last(i + 1, nv), 0, 0)),
                pl.BlockSpec(memory_space=pl.ANY),
                pl.BlockSpec((1, D_MODEL), lambda i, be, nv: (0, 0)),
                pl.BlockSpec((1, D_MODEL, EXPERT_FF), lambda i, be, nv: (be[last(i, nv)], 0, 0)),
                pl.BlockSpec((1, D_MODEL, EXPERT_FF), lambda i, be, nv: (be[last(i, nv)], 0, 0)),
                pl.BlockSpec((1, EXPERT_FF, D_MODEL), lambda i, be, nv: (be[last(i, nv)], 0, 0)),
            ],
            out_specs=pl.BlockSpec((tm, D_MODEL), lambda i, be, nv: (i, 0)),
            scratch_shapes=[pltpu.VMEM((2, tm, D_MODEL), F32), pltpu.SemaphoreType.DMA((2,))],
        ),
        out_shape=jax.ShapeDtypeStruct((nb * tm, D_MODEL), F32),
        compiler_params=pltpu.CompilerParams(
            dimension_semantics=("arbitrary",), vmem_limit_bytes=_vmem_limit(blocks, scratch_bytes)),
        name="experts",
    )(block_e, n_valid, rows_tok, rows_tok, h, g, w_g, w_u, w_d)


def _combine_kernel(dst_ref, dst_next_ref, yb_hbm, h_ref, gate_ref, y_ref, ybuf, sem):
    tt = h_ref.shape[0]
    n = TOP_K * tt
    i = pl.program_id(0)
    slot = lax.rem(i, 2)

    @pl.when(i == 0)
    def _():
        _row_gather(dst_ref, n, yb_hbm, ybuf.at[0], sem.at[0])

    @pl.when(i + 1 < pl.num_programs(0))
    def _():
        _row_gather(dst_next_ref, n, yb_hbm, ybuf.at[1 - slot], sem.at[1 - slot])

    pltpu.make_async_copy(yb_hbm.at[pl.ds(0, n)], ybuf.at[slot], sem.at[slot]).wait()
    gates = gate_ref[...]
    moe = gates[:, 0:1] * ybuf[slot, 0:tt, :]
    for k in range(1, TOP_K):
        moe = moe + gates[:, k:k + 1] * ybuf[slot, k * tt:(k + 1) * tt, :]
    y_ref[...] = h_ref[...] + moe


def _combine(dest_blocks, yb, h, gates, tt):
    T = h.shape[0]
    n = TOP_K * tt
    nblk = T // tt
    smem_blk = lambda f: pl.BlockSpec((1, 1, n), f, memory_space=pltpu.SMEM)
    blocks = 2 * _nbytes((tt, D_MODEL), F32) + _nbytes((tt, LANES), F32)
    scratch_bytes = 2 * _nbytes((n, D_MODEL), F32)
    return pl.pallas_call(
        _combine_kernel,
        grid=(nblk,),
        in_specs=[smem_blk(lambda i: (i, 0, 0)), smem_blk(lambda i: (jnp.minimum(i + 1, nblk - 1), 0, 0)),
                  pl.BlockSpec(memory_space=pl.ANY),
                  pl.BlockSpec((tt, D_MODEL), lambda i: (i, 0)), pl.BlockSpec((tt, LANES), lambda i: (i, 0))],
        out_specs=pl.BlockSpec((tt, D_MODEL), lambda i: (i, 0)),
        out_shape=jax.ShapeDtypeStruct((T, D_MODEL), F32),
        scratch_shapes=[pltpu.VMEM((2, n, D_MODEL), F32), pltpu.SemaphoreType.DMA((2,))],
        compiler_params=pltpu.CompilerParams(
            dimension_semantics=("arbitrary",), vmem_limit_bytes=_vmem_limit(blocks, scratch_bytes)),
        name="combine",
    )(dest_blocks, dest_blocks, yb, h, gates)


def _moe(h, gates, experts, g, w_g, w_u, w_d, tm, tt):
    T = h.shape[0]
    A = T * TOP_K
    flat_e = experts[:, :TOP_K].reshape(A)
    flat_tok = jnp.repeat(jnp.arange(T, dtype=jnp.int32), TOP_K)
    order = jnp.argsort(flat_e)
    se, stok = flat_e[order], flat_tok[order]
    counts = jnp.bincount(flat_e, length=N_EXPERTS)
    starts = jnp.cumsum(counts) - counts
    pcounts = (counts + tm - 1) // tm * tm
    pends = jnp.cumsum(pcounts)
    pstarts = pends - pcounts
    dest_sorted = (pstarts[se] + jnp.arange(A) - starts[se]).astype(jnp.int32)
    nb = -(-A // tm) + N_EXPERTS
    block_e = jnp.minimum(jnp.sum(pends[None, :] <= (jnp.arange(nb) * tm)[:, None], -1), N_EXPERTS - 1).astype(jnp.int32)
    first = jnp.clip(starts[block_e] + jnp.arange(nb) * tm - pstarts[block_e], 0, A).astype(jnp.int32)
    stok_pad = jnp.concatenate([stok, jnp.zeros((tm,), jnp.int32)])
    rows_tok = jax.vmap(lambda s: lax.dynamic_slice(stok_pad, (s,), (tm,)))(first)
    n_valid = (pends[-1:] // tm).astype(jnp.int32)
    yb = _experts(block_e, n_valid, rows_tok.reshape(nb, 1, tm), h, g, w_g, w_u, w_d, tm)
    dest = dest_sorted[jnp.argsort(order)].reshape(T // tt, tt, TOP_K)
    dest_blocks = dest.transpose(0, 2, 1).reshape(T // tt, 1, TOP_K * tt)
    return _combine(dest_blocks, yb, h, gates, tt)


def _rope_tables(pos):
    half = HEAD_DIM // 2
    inv = ROPE_THETA ** (-jnp.arange(half, dtype=F32) / half)
    ang = pos.astype(F32)[:, None] * inv[None, :]
    cos, sin = jnp.cos(ang), jnp.sin(ang)
    return jnp.concatenate([cos, cos], -1), jnp.concatenate([-sin, sin], -1)


def _layer(x, pos, w, tm, lt, chunk, moe_tm, moe_tt, conv_prev=None, s0=None, win_k=None, win_v=None):
    B, L, D = x.shape
    T = B * L
    x2d = x.reshape(T, D)
    z, zs = _in_proj(x2d, w['norm1_g'], w['w_main'], w['w_small'], min(2 * tm, T))
    z3 = z.reshape(B, L, MAIN_COLS)
    zs3 = zs.reshape(B, L, LANES)
    if conv_prev is None:
        o_a, s_new, conv_new = _delta_prompt(z3, zs3, w['conv_w'], w['a_vec'], w['dt_vec'], w['o_norm_g'], lt)
    else:
        o_a, s_new, conv_new = _delta_mixer(z3, zs3, w['conv_w'], w['a_vec'], w['dt_vec'], w['o_norm_g'], lt, chunk,
                                            conv_prev, s0)
    cos2, sin2 = _rope_tables(pos)
    if win_k is None:
        o_b, wk_new, wv_new = _attn_prompt(z3, cos2, sin2, w['q_norm_g'], w['k_norm_g'])
    else:
        rows = win_k.shape[1] * H_ATTN
        o_b, wk_new, wv_new = _attn_sample(z3, win_k.reshape(B, rows, HEAD_DIM), win_v.reshape(B, rows, HEAD_DIM),
                                           cos2, sin2, w['q_norm_g'], w['k_norm_g'])
    h, gates, experts = _out_proj(o_a.reshape(T, W_DELTA), o_b.reshape(T, W_ATTN), x2d, w['w_out_a'], w['w_out_b'],
                                  w['norm2_g'], w['w_rt_hi'], w['w_rt_lo'], w['b_rt'], tm)
    y = _moe(h, gates, experts, w['norm2_g'], w['w_g'], w['w_u'], w['w_d'], moe_tm, moe_tt)
    keep = wk_new.size // (B * W_ATTN)
    return (y.reshape(B, L, D), conv_new, s_new,
            wk_new.reshape(B, keep, H_ATTN, HEAD_DIM), wv_new.reshape(B, keep, H_ATTN, HEAD_DIM))


def _prep_weights(norm1_g, w_in, conv_w, A_log, dt_bias, o_norm_g, q_norm_g, k_norm_g, w_out, norm2_g,
                  w_group, b_group, w_router, b_router, w_exp_gate, w_exp_up, w_exp_down):
    sizes = [W_DELTA, W_DELTA, W_DELTA, H_DELTA, H_DELTA, W_DELTA, W_ATTN, W_ATTN, W_ATTN]
    zq, zk, zv, zb, za, zg, aq, ak, av = jnp.split(w_in, np.cumsum(sizes)[:-1].tolist(), axis=-1)
    pad = LANES - 2 * H_DELTA
    lane_vec = lambda v: jnp.pad(v.astype(F32), (H_DELTA, LANES - 2 * H_DELTA))[None, :]
    w_rt = jnp.pad(jnp.concatenate([w_group, w_router], -1), ((0, 0), (0, LANES - N_GROUPS - N_EXPERTS)))
    w_rt_hi = w_rt.astype(BF16)
    return dict(
        norm1_g=norm1_g[None, :],
        w_main=jnp.concatenate([zq, zk, zv, zg, aq, ak, av], -1).astype(BF16),
        w_small=jnp.pad(jnp.concatenate([zb, za], -1), ((0, 0), (0, pad))).astype(BF16),
        conv_w=conv_w,
        a_vec=lane_vec(A_log), dt_vec=lane_vec(dt_bias),
        o_norm_g=o_norm_g[None, :], q_norm_g=q_norm_g[None, :], k_norm_g=k_norm_g[None, :],
        w_out_a=w_out[:W_DELTA].astype(BF16), w_out_b=w_out[W_DELTA:].astype(BF16),
        norm2_g=norm2_g[None, :],
        w_rt_hi=w_rt_hi, w_rt_lo=(w_rt - w_rt_hi.astype(F32)).astype(BF16),
        b_rt=jnp.pad(jnp.concatenate([b_group, b_router], -1), (0, LANES - N_GROUPS - N_EXPERTS))[None, :],
        w_g=w_exp_gate.astype(BF16), w_u=w_exp_up.astype(BF16), w_d=w_exp_down.astype(BF16),
    )


def kernel(x_prompt, x_sample, state_conv, state_delta, cache_win_k, cache_win_v, norm1_g, w_in, conv_w, A_log, dt_bias, o_norm_g, q_norm_g, k_norm_g, w_out, norm2_g, w_group, b_group, w_router, b_router, w_exp_gate, w_exp_up, w_exp_down):
    depth = w_in.shape[0]
    pos_p = jnp.arange(x_prompt.shape[1], dtype=jnp.int32)
    pos_s = PAST_LEN + jnp.arange(x_sample.shape[1], dtype=jnp.int32)
    yp, ys = x_prompt, x_sample
    outs = [[] for _ in range(8)]
    for l in range(depth):
        w = _prep_weights(norm1_g[l], w_in[l], conv_w[l], A_log[l], dt_bias[l], o_norm_g[l], q_norm_g[l],
                          k_norm_g[l], w_out[l], norm2_g[l], w_group[l], b_group[l], w_router[l], b_router[l],
                          w_exp_gate[l], w_exp_up[l], w_exp_down[l])
        yp, c1, s1, k1, v1 = _layer(yp, pos_p, w, tm=512, lt=256, chunk=CHUNK, moe_tm=512, moe_tt=256)
        ls = x_sample.shape[1]
        ys, c2, s2, k2, v2 = _layer(ys, pos_s, w, tm=x_sample.shape[0] * ls, lt=ls, chunk=min(CHUNK, ls), moe_tm=128,
                                    moe_tt=128,
                                    conv_prev=state_conv[l], s0=state_delta[l],
                                    win_k=cache_win_k[l], win_v=cache_win_v[l])
        for lst, v in zip(outs, (c1, c2, s1, s2, k1, k2, v1, v2)):
            lst.append(v)
    return (yp, ys) + tuple(jnp.stack(o) for o in outs)
```

```python
import functools
import math

import numpy as np
import jax
import jax.numpy as jnp
from jax import lax
from jax.experimental import pallas as pl
from jax.experimental.pallas import tpu as pltpu

F32 = jnp.float32
BF16 = jnp.bfloat16
HIGHEST = lax.Precision.HIGHEST

D_MODEL = 2048
HEAD_DIM = 128
H_DELTA = 8
H_ATTN = 8
W_DELTA = H_DELTA * HEAD_DIM
W_ATTN = H_ATTN * HEAD_DIM
CONV_W = 4
CHUNK = 64
DIL_PAIRS = ((128, 1), (512, 4), (2048, 16))
ATT_BLOCK = 128
ROPE_THETA = 10000.0
N_GROUPS = 4
EXPERTS_PER_GROUP = 8
N_EXPERTS = N_GROUPS * EXPERTS_PER_GROUP
TOP_K = 2
EXPERT_FF = 768
EPS = 1e-6
PAST_LEN = 16384

LANES = 128
SUBLANES = 8
V7X_VMEM_BYTES = 64 * 1024 * 1024
COMPILER_TEMP_BYTES = 12 * 1024 * 1024

COL_ZQ, COL_ZK, COL_ZV, COL_ZG, COL_AQ, COL_AK, COL_AV = (i * H_DELTA for i in range(7))
MAIN_COLS = 7 * W_DELTA
ATTN_SCALE = HEAD_DIM ** -0.5


def _vmem_limit(block_bytes, scratch_bytes=0):
    return int(min(2 * block_bytes + scratch_bytes + COMPILER_TEMP_BYTES, V7X_VMEM_BYTES - 4 * 1024 * 1024))


def _nbytes(shape, dtype):
    return int(np.prod(shape)) * jnp.dtype(dtype).itemsize


def _rms(x, g):
    return x * lax.rsqrt(jnp.mean(x * x, -1, keepdims=True) + EPS) * g


def _sigmoid(x):
    return 1.0 / (1.0 + jnp.exp(-x))


def _dot_nt(a, b, **kw):
    return lax.dot_general(a, b, (((1,), (1,)), ((), ())), **kw)


def _dot_tn(a, b, **kw):
    return lax.dot_general(a, b, (((0,), (0,)), ((), ())), **kw)


def _in_proj_kernel(x_ref, g_ref, w_ref, ws_ref, z_ref, zs_ref, xn_ref):
    @pl.when(pl.program_id(1) == 0)
    def _():
        xn_ref[...] = _rms(x_ref[...], g_ref[...]).astype(BF16)
        zs_ref[...] = jnp.dot(xn_ref[...], ws_ref[...], preferred_element_type=F32)

    z_ref[...] = jnp.dot(xn_ref[...], w_ref[...], preferred_element_type=F32)


def _in_proj(x2d, g, w_main, w_small, tm, tn=1024):
    T = x2d.shape[0]
    blocks = (_nbytes((tm, D_MODEL), F32) + _nbytes((D_MODEL, tn), BF16) + _nbytes((D_MODEL, LANES), BF16)
              + _nbytes((tm, tn), F32) + _nbytes((tm, LANES), F32))
    return pl.pallas_call(
        _in_proj_kernel,
        grid=(T // tm, MAIN_COLS // tn),
        in_specs=[
            pl.BlockSpec((tm, D_MODEL), lambda i, j: (i, 0)),
            pl.BlockSpec((1, D_MODEL), lambda i, j: (0, 0)),
            pl.BlockSpec((D_MODEL, tn), lambda i, j: (0, j)),
            pl.BlockSpec((D_MODEL, LANES), lambda i, j: (0, 0)),
        ],
        out_specs=[
            pl.BlockSpec((tm, tn), lambda i, j: (i, j)),
            pl.BlockSpec((tm, LANES), lambda i, j: (i, 0)),
        ],
        out_shape=[jax.ShapeDtypeStruct((T, MAIN_COLS), F32), jax.ShapeDtypeStruct((T, LANES), F32)],
        scratch_shapes=[pltpu.VMEM((tm, D_MODEL), BF16)],
        compiler_params=pltpu.CompilerParams(
            dimension_semantics=("parallel", "arbitrary"),
            vmem_limit_bytes=_vmem_limit(blocks, _nbytes((tm, D_MODEL), BF16))),
        name="in_proj",
    )(x2d, g, w_main, w_small)


def _delta_chunk(cq, ck, cv, beta_b, g_b, S):
    C = cq.shape[0]
    q = cq * lax.rsqrt(jnp.sum(cq * cq, -1, keepdims=True) + EPS) * ATTN_SCALE
    k = ck * lax.rsqrt(jnp.sum(ck * ck, -1, keepdims=True) + EPS)
    row = lax.broadcasted_iota(jnp.int32, (C, C), 0)
    col = lax.broadcasted_iota(jnp.int32, (C, C), 1)
    eye = jnp.where(row == col, 1.0, 0.0)
    gam_b = _dot_sel(jnp.where(row >= col, 1.0, 0.0).astype(BF16), g_b)
    gam_col = gam_b[:, :C]
    gam_row = _dot_sel(jnp.ones((C, C), BF16), eye * gam_col)
    dec = jnp.exp(jnp.minimum(gam_col - gam_row, 0.0))
    dec_strict = jnp.where(row > col, dec, 0.0)
    dec_incl = jnp.where(row >= col, dec, 0.0)
    kq = _dot3(jnp.concatenate([k, q], 0), k, dot=_dot_nt)
    qk = kq[C:] * dec_incl
    x = -(beta_b[:, :C] * kq[:C] * dec_strict)
    inv = eye + x
    for _ in range(int(math.log2(C)) - 1):
        x = _dot3(x, x)
        inv = inv + _dot3(inv, x)
    eg = jnp.exp(gam_b)
    uw = _dot3(inv, jnp.concatenate([beta_b * cv, beta_b * eg * k], 1))
    u, w = uw[:, :HEAD_DIM], uw[:, HEAD_DIM:]
    gam_last = gam_b[C - 1:C, :]
    kdec = k * jnp.exp(gam_last - gam_b)
    wq_s = _dot3(jnp.concatenate([w, q], 0), S, dot=_dot_nt)
    delta = u - wq_s[:C]
    o = eg * wq_s[C:] + _dot3(qk, delta)
    s_new = jnp.exp(gam_last) * S + _dot_tn(delta, kdec, precision=HIGHEST, preferred_element_type=F32)
    return o, s_new


def _delta_kernel(*refs, lt, chunk, has_state):
    if has_state:
        (zq_ref, zk_ref, zv_ref, zg_ref, zs_ref, cw_ref, a_ref, dt_ref, og_ref, prev_ref, s0_ref,
         o_ref, sout_ref, cout_ref, u_s, c_s, s_s) = refs
    else:
        (zq_ref, zk_ref, zv_ref, zg_ref, zs_ref, cw_ref, a_ref, dt_ref, og_ref,
         o_ref, sout_ref, cout_ref, u_s, c_s, s_s) = refs
    l = pl.program_id(1)
    halo = SUBLANES
    width = 3 * W_DELTA

    @pl.when(l == 0)
    def _():
        if has_state:
            u_s[0:halo, :] = jnp.zeros((halo, width), F32)
            u_s[halo - (CONV_W - 1):halo, :] = prev_ref[0]
            s_s[...] = s0_ref[0]
        else:
            u_s[0:halo, :] = jnp.zeros((halo, width), F32)
            s_s[...] = jnp.zeros(s_s.shape, F32)

    @pl.when(l > 0)
    def _():
        u_s[0:halo, :] = u_s[lt:lt + halo, :]

    u_s[halo:halo + lt, 0:W_DELTA] = zq_ref[0]
    u_s[halo:halo + lt, W_DELTA:2 * W_DELTA] = zk_ref[0]
    u_s[halo:halo + lt, 2 * W_DELTA:width] = zv_ref[0]

    base = halo - (CONV_W - 1)
    acc = cw_ref[0:1, :] * u_s[base:base + lt, :]
    for i in range(1, CONV_W):
        acc = acc + cw_ref[i:i + 1, :] * u_s[base + i:base + i + lt, :]
    c_s[...] = acc * _sigmoid(acc)

    neg_a = -jnp.exp(a_ref[...])

    def chunk_body(ci, carry):
        r0 = pl.multiple_of(ci * chunk, chunk)
        zs = zs_ref[0, pl.ds(r0, chunk), :]
        beta_all = _sigmoid(zs)
        sp_in = zs + dt_ref[...]
        g_all = neg_a * (jnp.maximum(sp_in, 0.0) + jnp.log1p(jnp.exp(-jnp.abs(sp_in))))
        for h in range(H_DELTA):
            lo = h * HEAD_DIM
            cq = c_s[pl.ds(r0, chunk), lo:lo + HEAD_DIM]
            ck = c_s[pl.ds(r0, chunk), W_DELTA + lo:W_DELTA + lo + HEAD_DIM]
            cv = c_s[pl.ds(r0, chunk), 2 * W_DELTA + lo:2 * W_DELTA + lo + HEAD_DIM]
            beta_b = jnp.broadcast_to(beta_all[:, h:h + 1], (chunk, HEAD_DIM))
            g_b = jnp.broadcast_to(g_all[:, H_DELTA + h:H_DELTA + h + 1], (chunk, HEAD_DIM))
            o, s_new = _delta_chunk(cq, ck, cv, beta_b, g_b, s_s[h])
            s_s[h] = s_new
            zg = zg_ref[0, pl.ds(r0, chunk), lo:lo + HEAD_DIM]
            o = _rms(o, og_ref[...]) * (zg * _sigmoid(zg))
            o_ref[0, pl.ds(r0, chunk), lo:lo + HEAD_DIM] = o.astype(o_ref.dtype)
        return carry

    lax.fori_loop(0, lt // chunk, chunk_body, 0)

    @pl.when(l == pl.num_programs(1) - 1)
    def _():
        sout_ref[0] = s_s[...]
        cout_ref[0] = u_s[halo + lt - (CONV_W - 1):halo + lt, :]


def _delta_mixer(z3, zs3, conv_w, a_vec, dt_vec, o_norm_g, lt, chunk, conv_prev=None, s0=None):
    B, L, _ = z3.shape
    has_state = conv_prev is not None
    width = 3 * W_DELTA
    col_blk = lambda c: pl.BlockSpec((1, lt, W_DELTA), lambda b, l, c=c: (b, l, c))
    full = lambda shape: pl.BlockSpec(shape, lambda b, l: (0,) * len(shape))
    in_specs = [col_blk(0), col_blk(1), col_blk(2), col_blk(3),
                pl.BlockSpec((1, lt, LANES), lambda b, l: (b, l, 0)),
                full((CONV_W, width)), full((1, LANES)), full((1, LANES)), full((1, HEAD_DIM))]
    args = [z3, z3, z3, z3, zs3, conv_w, a_vec, dt_vec, o_norm_g]
    if has_state:
        in_specs += [pl.BlockSpec((1, CONV_W - 1, width), lambda b, l: (b, 0, 0)),
                     pl.BlockSpec((1, H_DELTA, HEAD_DIM, HEAD_DIM), lambda b, l: (b, 0, 0, 0))]
        args += [conv_prev, s0]
    blocks = (4 * _nbytes((lt, W_DELTA), F32) + _nbytes((lt, LANES), F32) + _nbytes((CONV_W, width), F32)
              + _nbytes((lt, W_DELTA), BF16) + 2 * _nbytes((H_DELTA, HEAD_DIM, HEAD_DIM), F32)
              + 2 * _nbytes((SUBLANES, width), F32))
    scratch = [pltpu.VMEM((SUBLANES + lt, width), F32), pltpu.VMEM((lt, width), F32),
               pltpu.VMEM((H_DELTA, HEAD_DIM, HEAD_DIM), F32)]
    scratch_bytes = (_nbytes((SUBLANES + lt, width), F32) + _nbytes((lt, width), F32)
                     + _nbytes((H_DELTA, HEAD_DIM, HEAD_DIM), F32))
    return pl.pallas_call(
        functools.partial(_delta_kernel, lt=lt, chunk=chunk, has_state=has_state),
        grid=(B, L // lt),
        in_specs=in_specs,
        out_specs=[
            pl.BlockSpec((1, lt, W_DELTA), lambda b, l: (b, l, 0)),
            pl.BlockSpec((1, H_DELTA, HEAD_DIM, HEAD_DIM), lambda b, l: (b, 0, 0, 0)),
            pl.BlockSpec((1, CONV_W - 1, width), lambda b, l: (b, 0, 0)),
        ],
        out_shape=[jax.ShapeDtypeStruct((B, L, W_DELTA), BF16),
                   jax.ShapeDtypeStruct((B, H_DELTA, HEAD_DIM, HEAD_DIM), F32),
                   jax.ShapeDtypeStruct((B, CONV_W - 1, width), F32)],
        scratch_shapes=scratch,
        compiler_params=pltpu.CompilerParams(
            dimension_semantics=("parallel", "arbitrary"),
            vmem_limit_bytes=_vmem_limit(blocks, scratch_bytes)),
        name="delta_mixer",
    )(*args)


def _split_bf16(x):
    hi = x.astype(BF16)
    return hi, (x - hi.astype(F32)).astype(BF16)


def _dot3(a, b, dot=jnp.dot):
    ah, al = _split_bf16(a)
    bh, bl = _split_bf16(b)
    kw = dict(preferred_element_type=F32)
    if a.shape[-1] == LANES:
        b_axis = 0 if dot is jnp.dot else 1
        return dot(jnp.concatenate([ah, ah, al], 1), jnp.concatenate([bh, bl, bh], b_axis), **kw)
    return dot(ah, bh, **kw) + dot(ah, bl, **kw) + dot(al, bh, **kw)


def _dot_sel(sel, x):
    x1 = x.astype(BF16)
    r1 = x - x1.astype(F32)
    x2 = r1.astype(BF16)
    x3 = (r1 - x2.astype(F32)).astype(BF16)
    kw = dict(preferred_element_type=F32)
    return jnp.dot(sel, x1, **kw) + jnp.dot(sel, x2, **kw) + jnp.dot(sel, x3, **kw)


def _beta_g(zs, neg_a, dt):
    sp_in = zs + dt
    return _sigmoid(zs), neg_a * (jnp.maximum(sp_in, 0.0) + jnp.log1p(jnp.exp(-jnp.abs(sp_in))))


def _delta_prompt_kernel(zq_ref, zk_ref, zv_ref, zg_ref, zs_ref, cw_ref, a_ref, dt_ref, og_ref,
                         o_ref, sout_ref, cout_ref,
                         u_s, c_s, s_s, uu_s, ww_s, qe_s, kd_s, qk_s, egl_s, *, lt):
    C = CHUNK
    P2 = 2 * C
    n_pairs = H_DELTA // 2
    l = pl.program_id(1)
    halo = SUBLANES
    width = 3 * W_DELTA

    @pl.when(l == 0)
    def _():
        u_s[0:halo, :] = jnp.zeros((halo, width), F32)
        s_s[...] = jnp.zeros(s_s.shape, F32)

    @pl.when(l > 0)
    def _():
        u_s[0:halo, :] = u_s[lt:lt + halo, :]

    u_s[halo:halo + lt, 0:W_DELTA] = zq_ref[0]
    u_s[halo:halo + lt, W_DELTA:2 * W_DELTA] = zk_ref[0]
    u_s[halo:halo + lt, 2 * W_DELTA:width] = zv_ref[0]

    base = halo - (CONV_W - 1)
    acc = cw_ref[0:1, :] * u_s[base:base + lt, :]
    for i in range(1, CONV_W):
        acc = acc + cw_ref[i:i + 1, :] * u_s[base + i:base + i + lt, :]
    c_s[...] = acc * _sigmoid(acc)

    neg_a = -jnp.exp(a_ref[...])
    row = lax.broadcasted_iota(jnp.int32, (P2, P2), 0)
    col = lax.broadcasted_iota(jnp.int32, (P2, P2), 1)
    same_head = (row & C) == (col & C)
    m_strict = jnp.where(same_head & (row > col), 1.0, 0.0)
    m_incl = jnp.where(same_head & (row >= col), 1.0, 0.0)
    eye = jnp.where(row == col, 1.0, 0.0)
    tri = jnp.where(lax.broadcasted_iota(jnp.int32, (C, C), 0) >= lax.broadcasted_iota(jnp.int32, (C, C), 1),
                    1.0, 0.0).astype(BF16)
    lane = lax.broadcasted_iota(jnp.int32, (1, P2), 1)
    head_lanes = lambda h: slice(h * HEAD_DIM, (h + 1) * HEAD_DIM)
    pairs = range(n_pairs)

    def phase_a(ci, carry):
        rows = pl.ds(pl.multiple_of(ci * C, C), C)
        beta_all, g_all = _beta_g(zs_ref[0, rows, :], neg_a, dt_ref[...])
        gam_all = _dot_sel(tri, g_all)
        gam_t = jnp.concatenate([gam_all, gam_all], 0).T

        k2, q2, rhs2, beta2, gcol, grow, qe, kdec = [], [], [], [], [], [], [], []
        for p in pairs:
            ks, qs, vs, bs, gs = [], [], [], [], []
            for h in (2 * p, 2 * p + 1):
                cq = c_s[rows, head_lanes(h)]
                ck = c_s[rows, W_DELTA + h * HEAD_DIM:W_DELTA + (h + 1) * HEAD_DIM]
                cv = c_s[rows, 2 * W_DELTA + h * HEAD_DIM:2 * W_DELTA + (h + 1) * HEAD_DIM]
                q = cq * lax.rsqrt(jnp.sum(cq * cq, -1, keepdims=True) + EPS) * ATTN_SCALE
                k = ck * lax.rsqrt(jnp.sum(ck * ck, -1, keepdims=True) + EPS)
                gam_b = jnp.broadcast_to(gam_all[:, H_DELTA + h:H_DELTA + h + 1], (C, HEAD_DIM))
                gam_last = gam_b[C - 1:C, :]
                kd_s[rows, head_lanes(h)] = k * jnp.exp(gam_last - gam_b)
                egl_s[pl.ds(ci * H_DELTA + h, 1), :] = jnp.exp(gam_last)
                ks.append(k)
                qs.append(q)
                vs.append(cv)
                bs.append(jnp.broadcast_to(beta_all[:, h:h + 1], (C, HEAD_DIM)))
                gs.append(gam_b)
            k2.append(jnp.concatenate(ks, 0))
            q2.append(jnp.concatenate(qs, 0))
            beta2.append(jnp.concatenate(bs, 0))
            gcol.append(jnp.concatenate(gs, 0))
            grow.append(jnp.where(lane < C, gam_t[H_DELTA + 2 * p:H_DELTA + 2 * p + 1, :],
                                  gam_t[H_DELTA + 2 * p + 1:H_DELTA + 2 * p + 2, :]))
            eg = jnp.exp(gcol[p])
            qe.append(eg * q2[p])
            rhs2.append(jnp.concatenate([beta2[p] * jnp.concatenate(vs, 0), beta2[p] * eg * k2[p]], 1))

        kq = [_dot3(jnp.concatenate([k2[p], q2[p]], 0), k2[p], dot=_dot_nt) for p in pairs]
        dec = [jnp.exp(jnp.minimum(gcol[p] - grow[p], 0.0)) for p in pairs]
        for p in pairs:
            qk_s[ci * n_pairs + p] = kq[p][P2:] * (dec[p] * m_incl)
        xs = [-(beta2[p] * kq[p][:P2] * (dec[p] * m_strict)) for p in pairs]
        invs = [eye + x for x in xs]
        xs = [_dot3(x, x) for x in xs]
        for _ in range(int(math.log2(C)) - 2):
            ys = [_dot3(jnp.concatenate([inv, x], 0), x) for inv, x in zip(invs, xs)]
            invs = [inv + y[:P2] for inv, y in zip(invs, ys)]
            xs = [y[P2:] for y in ys]
        invs = [inv + _dot3(inv, x) for inv, x in zip(invs, xs)]
        uw = [_dot3(invs[p], rhs2[p]) for p in pairs]
        for p in pairs:
            for hh in range(2):
                h = 2 * p + hh
                uu_s[rows, head_lanes(h)] = uw[p][hh * C:(hh + 1) * C, :HEAD_DIM]
                ww_s[rows, head_lanes(h)] = uw[p][hh * C:(hh + 1) * C, HEAD_DIM:]
                qe_s[rows, head_lanes(h)] = qe[p][hh * C:(hh + 1) * C, :]
        return carry

    lax.fori_loop(0, lt // C, phase_a, 0)

    def phase_b(ci, carry):
        rows = pl.ds(pl.multiple_of(ci * C, C), C)
        heads = range(H_DELTA)
        wq_s = [_dot3(jnp.concatenate([ww_s[rows, head_lanes(h)], qe_s[rows, head_lanes(h)]], 0), s_s[h], dot=_dot_nt)
                for h in heads]
        delta = [uu_s[rows, head_lanes(h)] - wq_s[h][:C] for h in heads]
        od = [_dot3(qk_s[ci * n_pairs + p], jnp.concatenate([delta[2 * p], delta[2 * p + 1]], 0)) for p in pairs]
        for h in heads:
            o = wq_s[h][C:] + od[h // 2][(h % 2) * C:(h % 2 + 1) * C]
            zg = zg_ref[0, rows, head_lanes(h)]
            o_ref[0, rows, head_lanes(h)] = (_rms(o, og_ref[...]) * (zg * _sigmoid(zg))).astype(o_ref.dtype)
        for h in heads:
            s_s[h] = egl_s[pl.ds(ci * H_DELTA + h, 1), :] * s_s[h] + _dot3(delta[h].T, kd_s[rows, head_lanes(h)])
        return carry

    lax.fori_loop(0, lt // C, phase_b, 0)

    @pl.when(l == pl.num_programs(1) - 1)
    def _():
        sout_ref[0] = s_s[...]
        cout_ref[0] = u_s[halo + lt - (CONV_W - 1):halo + lt, :]


def _delta_prompt(z3, zs3, conv_w, a_vec, dt_vec, o_norm_g, lt):
    B, L, _ = z3.shape
    width = 3 * W_DELTA
    n_chunks = lt // CHUNK
    col_blk = lambda c: pl.BlockSpec((1, lt, W_DELTA), lambda b, l, c=c: (b, l, c))
    full = lambda shape: pl.BlockSpec(shape, lambda b, l: (0,) * len(shape))
    blocks = (4 * _nbytes((lt, W_DELTA), F32) + _nbytes((lt, LANES), F32) + _nbytes((CONV_W, width), F32)
              + _nbytes((lt, W_DELTA), BF16) + _nbytes((H_DELTA, HEAD_DIM, HEAD_DIM), F32)
              + _nbytes((SUBLANES, width), F32))
    scratch_dims = [(SUBLANES + lt, width), (lt, width), (H_DELTA, HEAD_DIM, HEAD_DIM),
                    (lt, W_DELTA), (lt, W_DELTA), (lt, W_DELTA), (lt, W_DELTA),
                    (n_chunks * H_DELTA // 2, 2 * CHUNK, 2 * CHUNK), (n_chunks * H_DELTA, HEAD_DIM)]
    return pl.pallas_call(
        functools.partial(_delta_prompt_kernel, lt=lt),
        grid=(B, L // lt),
        in_specs=[col_blk(0), col_blk(1), col_blk(2), col_blk(3),
                  pl.BlockSpec((1, lt, LANES), lambda b, l: (b, l, 0)),
                  full((CONV_W, width)), full((1, LANES)), full((1, LANES)), full((1, HEAD_DIM))],
        out_specs=[
            pl.BlockSpec((1, lt, W_DELTA), lambda b, l: (b, l, 0)),
            pl.BlockSpec((1, H_DELTA, HEAD_DIM, HEAD_DIM), lambda b, l: (b, 0, 0, 0)),
            pl.BlockSpec((1, CONV_W - 1, width), lambda b, l: (b, 0, 0)),
        ],
        out_shape=[jax.ShapeDtypeStruct((B, L, W_DELTA), BF16),
                   jax.ShapeDtypeStruct((B, H_DELTA, HEAD_DIM, HEAD_DIM), F32),
                   jax.ShapeDtypeStruct((B, CONV_W - 1, width), F32)],
        scratch_shapes=[pltpu.VMEM(d, F32) for d in scratch_dims],
        compiler_params=pltpu.CompilerParams(
            dimension_semantics=("parallel", "arbitrary"),
            vmem_limit_bytes=_vmem_limit(blocks, sum(_nbytes(d, F32) for d in scratch_dims))),
        name="delta_prompt",
    )(z3, z3, z3, z3, zs3, conv_w, a_vec, dt_vec, o_norm_g)


def _norm_rot(x, g, cos2, sin2):
    xn = _rms(x, g)
    return xn * cos2 + pltpu.roll(xn, HEAD_DIM // 2, 1) * sin2


def _softmax_parts(s, mask, vb):
    s = jnp.where(mask, s, -jnp.inf)
    m = jnp.max(s, -1, keepdims=True)
    p = jnp.exp(s - m)
    l = jnp.sum(p, -1, keepdims=True)
    o = jnp.dot(p.astype(BF16), vb.astype(BF16), preferred_element_type=F32)
    return m, l, o


def _merge(parts):
    m_all = functools.reduce(jnp.maximum, [p[0] for p in parts])
    ws = [jnp.exp(p[0] - m_all) for p in parts]
    num = sum(w * p[2] for w, p in zip(ws, parts))
    den = sum(w * p[1] for w, p in zip(ws, parts))
    return num / den


ATT_QSUB = 64


def _band_mask(nq, nk, shift, window):
    dist = (lax.broadcasted_iota(jnp.int32, (nq, nk), 0) + shift) - lax.broadcasted_iota(jnp.int32, (nq, nk), 1)
    return (dist >= 0) & (dist <= window)
def _attn_prompt_kernel(aq_ref, ak_ref, av_ref, cos_ref, sin_ref, qg_ref, kg_ref,
                        ob_ref, kout_ref, vout_ref, q_s, k_s, qd_s, kd_s, vd_s, o_s, m_s, l_s):
    L = q_s.shape[0]
    blk = ATT_BLOCK
    rows = 256

    def prep(i, carry):
        r = pl.multiple_of(i * rows, rows)
        sl = pl.ds(r, rows)
        cos2, sin2 = cos_ref[sl, :], sin_ref[sl, :]
        q = _norm_rot(aq_ref[0, sl, :], qg_ref[...], cos2, sin2) * ATTN_SCALE
        k = _norm_rot(ak_ref[0, sl, :], kg_ref[...], cos2, sin2)
        v = av_ref[0, sl, :]
        q_s[sl, :] = q
        k_s[sl, :] = k
        kout_ref[0, sl, :] = k
        vout_ref[0, sl, :] = v
        qd_s[0, sl, :] = q.astype(BF16)
        kd_s[0, sl, :] = k.astype(BF16)
        vd_s[0, sl, :] = v.astype(BF16)
        return carry

    lax.fori_loop(0, L // rows, prep, 0, unroll=4)

    for branch, (window, dil) in enumerate(DIL_PAIRS):
        assert window // dil == blk
        n = L // dil
        for r in range(dil if dil > 1 else 0):
            src = pl.ds(r, n, stride=dil)
            dst = pl.ds(r * n, n)
            qd_s[branch, dst, :] = q_s[src, :].astype(BF16)
            kd_s[branch, dst, :] = k_s[src, :].astype(BF16)
            vd_s[branch, dst, :] = av_ref.at[0][src, :].astype(BF16)

    qs = ATT_QSUB
    band_mask = _band_mask(qs, qs + blk, blk, blk)
    first_masks = [_band_mask(qs, qo + qs, qo, blk) for qo in range(0, blk, qs)]

    def tiles(dil, r, i, first):
        n = L // dil
        out = []
        for qo in range(0, blk, qs):
            q0 = r * n + i * blk + qo
            k0, nk, mask = (q0 - qo, qo + qs, first_masks[qo // qs]) if first else (q0 - blk, qs + blk, band_mask)
            if not isinstance(q0, int):
                q0, k0 = pl.multiple_of(q0, qs), pl.multiple_of(k0, qs)
            out.append((q0, k0, nk, mask, pl.ds(r + dil * (blk * i + qo), qs, stride=dil if dil > 1 else None)))
        return out

    def run(branch, blocks):
        ts = [t for blk_args in blocks for t in tiles(*blk_args)]
        s = [_dot_nt(qd_s[branch, pl.ds(q0, qs), :], kd_s[branch, pl.ds(k0, nk), :], preferred_element_type=F32)
             for q0, k0, nk, _, _ in ts]
        s = [jnp.where(t[3], x, -jnp.inf) for t, x in zip(ts, s)]
        m = [jnp.max(x, -1, keepdims=True) for x in s]
        p = [jnp.exp(x - mx) for x, mx in zip(s, m)]
        l = [jnp.sum(x, -1, keepdims=True) for x in p]
        o = [jnp.dot(x.astype(BF16), vd_s[branch, pl.ds(t[1], t[2]), :], preferred_element_type=F32)
             for t, x in zip(ts, p)]
        for t, mx, lx, ox in zip(ts, m, l, o):
            o_s.at[branch][t[4], :] = ox
            m_s.at[branch][t[4], :] = jnp.broadcast_to(mx, (qs, HEAD_DIM))
            l_s.at[branch][t[4], :] = jnp.broadcast_to(lx, (qs, HEAD_DIM))

    group = 5
    for branch, (window, dil) in enumerate(DIL_PAIRS):
        nb = L // dil // blk
        if dil == 1:
            assert (nb - 1) % group == 0
            run(branch, [(dil, 0, 0, True)])

            def later(g, carry, branch=branch, dil=dil):
                run(branch, [(dil, 0, 1 + g * group + u, False) for u in range(group)])
                return carry

            lax.fori_loop(0, (nb - 1) // group, later, 0)
        else:
            per_body = max(1, 16 // nb)

            def residues(g, carry, branch=branch, dil=dil, nb=nb, per_body=per_body):
                run(branch, [(dil, g * per_body + u, i, i == 0) for u in range(per_body) for i in range(nb)])
                return carry

            lax.fori_loop(0, dil // per_body, residues, 0)

    def merge(i, carry):
        sl = pl.ds(pl.multiple_of(i * rows, rows), rows)
        parts = [(m_s[b, sl, :], l_s[b, sl, :], o_s[b, sl, :]) for b in range(len(DIL_PAIRS))]
        ob_ref[0, sl, :] = _merge(parts).astype(ob_ref.dtype)
        return carry

    lax.fori_loop(0, L // rows, merge, 0)


def _attn_prompt(z3, cos2, sin2, q_norm_g, k_norm_g):
    B, L, _ = z3.shape
    nbr = len(DIL_PAIRS)
    head_blk = lambda c0: pl.BlockSpec((1, L, HEAD_DIM), lambda b, h, c0=c0: (b, 0, c0 + h))
    out_blk = pl.BlockSpec((1, L, HEAD_DIM), lambda b, h: (b, 0, h))
    tab = pl.BlockSpec((L, HEAD_DIM), lambda b, h: (0, 0))
    vec = pl.BlockSpec((1, HEAD_DIM), lambda b, h: (0, 0))
    blocks = 5 * _nbytes((L, HEAD_DIM), F32) + 2 * _nbytes((L, HEAD_DIM), F32) + _nbytes((L, HEAD_DIM), BF16)
    scratch_bytes = (2 + 3 * nbr) * _nbytes((L, HEAD_DIM), F32) + 3 * nbr * _nbytes((L, HEAD_DIM), BF16)
    return pl.pallas_call(
        _attn_prompt_kernel,
        grid=(B, H_ATTN),
        in_specs=[head_blk(COL_AQ), head_blk(COL_AK), head_blk(COL_AV), tab, tab, vec, vec],
        out_specs=[out_blk, out_blk, out_blk],
        out_shape=[jax.ShapeDtypeStruct((B, L, W_ATTN), BF16),
                   jax.ShapeDtypeStruct((B, L, W_ATTN), F32),
                   jax.ShapeDtypeStruct((B, L, W_ATTN), F32)],
        scratch_shapes=[pltpu.VMEM((L, HEAD_DIM), F32), pltpu.VMEM((L, HEAD_DIM), F32),
                        pltpu.VMEM((nbr, L, HEAD_DIM), BF16), pltpu.VMEM((nbr, L, HEAD_DIM), BF16),
                        pltpu.VMEM((nbr, L, HEAD_DIM), BF16),
                        pltpu.VMEM((nbr, L, HEAD_DIM), F32), pltpu.VMEM((nbr, L, HEAD_DIM), F32),
                        pltpu.VMEM((nbr, L, HEAD_DIM), F32)],
        compiler_params=pltpu.CompilerParams(
            dimension_semantics=("parallel", "parallel"),
            vmem_limit_bytes=_vmem_limit(blocks, scratch_bytes)),
        name="attn_prompt",
    )(z3, z3, z3, cos2, sin2, q_norm_g, k_norm_g)


def _attn_sample_kernel(aq_ref, ak_ref, av_ref, ck_ref, cv_ref, ckh_ref, cvh_ref, cos_ref, sin_ref, qg_ref, kg_ref,
                        ob_ref, wk_ref, wv_ref, q_s, kn_s, m_s, l_s, acc_s):
    T = aq_ref.shape[1]
    H = H_ATTN
    rows = ck_ref.shape[1]
    P = rows // H
    j = pl.program_id(1)
    nj = pl.num_programs(1)
    wbuf = P * nj
    shift = T * H
    nbr = len(DIL_PAIRS)
    head_lanes = lambda h: slice(h * HEAD_DIM, (h + 1) * HEAD_DIM)

    @pl.when(j == 0)
    def _():
        cos2, sin2 = cos_ref[...], sin_ref[...]
        for h in range(H):
            q_s[h] = _norm_rot(aq_ref[0, :, head_lanes(h)], qg_ref[...], cos2, sin2)
            kn_s[h] = _norm_rot(ak_ref[0, :, head_lanes(h)], kg_ref[...], cos2, sin2)
        m_s[...] = jnp.full(m_s.shape, -jnp.inf, F32)
        l_s[...] = jnp.zeros(l_s.shape, F32)
        acc_s[...] = jnp.zeros(acc_s.shape, F32)

    def update(keys, vals, masks):
        units = [(br, h) for h in range(H) for br in range(nbr)]
        s = [_dot_nt(q_s[h].astype(BF16), keys[h].astype(BF16), preferred_element_type=F32) * ATTN_SCALE
             for h in range(H)]
        vb = [v.astype(BF16) for v in vals]
        sm = [jnp.where(masks[br], s[h], -jnp.inf) for br, h in units]
        m_old = [m_s[br, h] for br, h in units]
        m_new = [jnp.maximum(mo, jnp.max(x, -1, keepdims=True)) for mo, x in zip(m_old, sm)]
        m_safe = [jnp.where(mn == -jnp.inf, 0.0, mn) for mn in m_new]
        p = [jnp.exp(x - ms) for x, ms in zip(sm, m_safe)]
        alpha = [jnp.exp(mo - ms) for mo, ms in zip(m_old, m_safe)]
        pv = [jnp.dot(x.astype(BF16), vb[h], preferred_element_type=F32) for x, (br, h) in zip(p, units)]
        for u, (br, h) in enumerate(units):
            l_s[br, h] = alpha[u] * l_s[br, h] + jnp.sum(p[u], -1, keepdims=True)
            acc_s[br, h] = alpha[u] * acc_s[br, h] + pv[u]
            m_s[br, h] = m_new[u]

    t_c = lax.broadcasted_iota(jnp.int32, (T, P), 0)
    dist_c = wbuf + t_c - (j * P + lax.broadcasted_iota(jnp.int32, (T, P), 1))
    update([ck_ref.at[0][pl.ds(h, P, stride=H), :] for h in range(H)],
           [cv_ref.at[0][pl.ds(h, P, stride=H), :] for h in range(H)],
           [((dist_c & (dil - 1)) == 0) & (dist_c <= window) for window, dil in DIL_PAIRS])

    wk_ref[0, 0:rows - shift, :] = ck_ref[0, shift:rows, :]
    wv_ref[0, 0:rows - shift, :] = cv_ref[0, shift:rows, :]

    @pl.when(j < nj - 1)
    def _():
        wk_ref[0, rows - shift:rows, :] = ckh_ref[0]
        wv_ref[0, rows - shift:rows, :] = cvh_ref[0]

    @pl.when(j == nj - 1)
    def _():
        t_n = lax.broadcasted_iota(jnp.int32, (T, T), 0)
        dist_n = t_n - lax.broadcasted_iota(jnp.int32, (T, T), 1)
        update([kn_s[h] for h in range(H)], [av_ref[0, :, head_lanes(h)] for h in range(H)],
               [(dist_n >= 0) & ((dist_n & (dil - 1)) == 0) & (dist_n <= window) for window, dil in DIL_PAIRS])
        for h in range(H):
            parts = [(m_s[br, h], l_s[br, h], acc_s[br, h]) for br in range(nbr)]
            ob_ref[0, :, head_lanes(h)] = _merge(parts).astype(ob_ref.dtype)
            wk_ref.at[0][pl.ds(rows - shift + h, T, stride=H), :] = kn_s[h]
            wv_ref.at[0][pl.ds(rows - shift + h, T, stride=H), :] = av_ref[0, :, head_lanes(h)]


def _attn_sample(z3, cache_k, cache_v, cos2, sin2, q_norm_g, k_norm_g, pos_chunk=1024):
    B, T, _ = z3.shape
    rows_all = cache_k.shape[1]
    rows = pos_chunk * H_ATTN
    nj = rows_all // rows
    shift = T * H_ATTN
    per_chunk = rows // shift
    zblk = lambda c0: pl.BlockSpec((1, T, W_ATTN), lambda b, j, c0=c0: (b, 0, c0 // H_ATTN))
    cblk = pl.BlockSpec((1, rows, HEAD_DIM), lambda b, j: (b, j, 0))
    halo = pl.BlockSpec((1, shift, HEAD_DIM), lambda b, j: (b, jnp.minimum((j + 1) * per_chunk, nj * per_chunk - 1), 0))
    tab = pl.BlockSpec((T, HEAD_DIM), lambda b, j: (0, 0))
    vec = pl.BlockSpec((1, HEAD_DIM), lambda b, j: (0, 0))
    nbr = len(DIL_PAIRS)
    blocks = 4 * _nbytes((rows, HEAD_DIM), F32) + 2 * _nbytes((shift, HEAD_DIM), F32) + 4 * _nbytes((T, W_ATTN), F32)
    scratch_dims = [(H_ATTN, T, HEAD_DIM), (H_ATTN, T, HEAD_DIM), (nbr, H_ATTN, T, 1), (nbr, H_ATTN, T, 1),
                    (nbr, H_ATTN, T, HEAD_DIM)]
    return pl.pallas_call(
        _attn_sample_kernel,
        grid=(B, nj),
        in_specs=[zblk(COL_AQ), zblk(COL_AK), zblk(COL_AV), cblk, cblk, halo, halo, tab, tab, vec, vec],
        out_specs=[pl.BlockSpec((1, T, W_ATTN), lambda b, j: (b, 0, 0)), cblk, cblk],
        out_shape=[jax.ShapeDtypeStruct((B, T, W_ATTN), BF16),
                   jax.ShapeDtypeStruct((B, rows_all, HEAD_DIM), F32),
                   jax.ShapeDtypeStruct((B, rows_all, HEAD_DIM), F32)],
        scratch_shapes=[pltpu.VMEM(d, F32) for d in scratch_dims],
        compiler_params=pltpu.CompilerParams(
            dimension_semantics=("parallel", "arbitrary"),
            vmem_limit_bytes=_vmem_limit(blocks, sum(_nbytes(d, F32) for d in scratch_dims))),
        name="attn_sample",
    )(z3, z3, z3, cache_k, cache_v, cache_k, cache_v, cos2, sin2, q_norm_g, k_norm_g)


def _route(logits):
    lane = lax.broadcasted_iota(jnp.int32, logits.shape, 1).astype(F32)
    first_max = lambda p, top, ok: jnp.min(jnp.where(ok & (p == top), lane, float(LANES)), -1, keepdims=True)
    is_group = lane < N_GROUPS
    gl = jnp.where(is_group, logits, -jnp.inf)
    pg = jnp.exp(gl - jnp.max(gl, -1, keepdims=True))
    pg = pg / jnp.sum(pg, -1, keepdims=True)
    pg_top = jnp.max(pg, -1, keepdims=True)
    g_top = first_max(pg, pg_top, is_group)
    lo = N_GROUPS + g_top * EXPERTS_PER_GROUP
    in_group = (lane >= lo) & (lane < lo + EXPERTS_PER_GROUP)
    el = jnp.where(in_group, logits, -jnp.inf)
    pe = jnp.exp(el - jnp.max(el, -1, keepdims=True))
    pe = pe / jnp.sum(pe, -1, keepdims=True)
    p1 = jnp.max(pe, -1, keepdims=True)
    l1 = first_max(pe, p1, in_group)
    rest = in_group & (lane != l1)
    p2 = jnp.max(jnp.where(rest, pe, -1.0), -1, keepdims=True)
    l2 = first_max(pe, p2, rest)
    denom = p1 + p2
    gates = jnp.where(lane == 0.0, pg_top * p1 / denom, jnp.where(lane == 1.0, pg_top * p2 / denom, 0.0))
    experts = jnp.where(lane == 0.0, l1 - N_GROUPS, jnp.where(lane == 1.0, l2 - N_GROUPS, 0.0))
    return gates, experts.astype(jnp.int32)


def _out_proj_kernel(oa_ref, ob_ref, x_ref, wa_ref, wb_ref, g_ref, wrh_ref, wrl_ref, br_ref, h_ref, gate_ref, exp_ref):
    h = (x_ref[...] + jnp.dot(oa_ref[...], wa_ref[...], preferred_element_type=F32)
         + jnp.dot(ob_ref[...], wb_ref[...], preferred_element_type=F32))
    h_ref[...] = h
    hn_hi, hn_lo = _split_bf16(_rms(h, g_ref[...]))
    w_hi, w_lo = wrh_ref[...], wrl_ref[...]
    logits = (jnp.dot(hn_hi, w_hi, preferred_element_type=F32) + jnp.dot(hn_hi, w_lo, preferred_element_type=F32)
              + jnp.dot(hn_lo, w_hi, preferred_element_type=F32)) + br_ref[...]
    gate_ref[...], exp_ref[...] = _route(logits)


def _out_proj(o_a, o_b, x2d, w_a, w_b, g, w_rt_hi, w_rt_lo, b_rt, tm):
    T = x2d.shape[0]
    row = lambda w: pl.BlockSpec((tm, w), lambda i: (i, 0))
    full = lambda shape: pl.BlockSpec(shape, lambda i: (0, 0))
    blocks = (2 * _nbytes((tm, W_DELTA), BF16) + 2 * _nbytes((tm, D_MODEL), F32) + 2 * _nbytes((W_DELTA, D_MODEL), BF16)
              + 2 * _nbytes((D_MODEL, LANES), BF16) + 2 * _nbytes((tm, LANES), F32))
    return pl.pallas_call(
        _out_proj_kernel,
        grid=(T // tm,),
        in_specs=[row(W_DELTA), row(W_ATTN), row(D_MODEL), full((W_DELTA, D_MODEL)), full((W_ATTN, D_MODEL)),
                  full((1, D_MODEL)), full((D_MODEL, LANES)), full((D_MODEL, LANES)), full((1, LANES))],
        out_specs=[row(D_MODEL), row(LANES), row(LANES)],
        out_shape=[jax.ShapeDtypeStruct((T, D_MODEL), F32), jax.ShapeDtypeStruct((T, LANES), F32),
                   jax.ShapeDtypeStruct((T, LANES), jnp.int32)],
        compiler_params=pltpu.CompilerParams(
            dimension_semantics=("parallel",), vmem_limit_bytes=_vmem_limit(blocks)),
        name="out_proj",
    )(o_a, o_b, x2d, w_a, w_b, g, w_rt_hi, w_rt_lo, b_rt)


def _row_gather(idx_ref, n, src_hbm, dst, sem):
    for r in range(n):
        pltpu.make_async_copy(src_hbm.at[pl.ds(idx_ref[0, 0, r], 1)], dst.at[pl.ds(r, 1)], sem).start()


def _expert_kernel(be_ref, nv_ref, tok_ref, tok_next_ref, h_hbm, g_ref, wg_ref, wu_ref, wd_ref, y_ref, xbuf, sem):
    del be_ref
    tm = xbuf.shape[1]
    i = pl.program_id(0)
    n_valid = nv_ref[0]
    slot = lax.rem(i, 2)

    @pl.when((i == 0) & (n_valid > 0))
    def _():
        _row_gather(tok_ref, tm, h_hbm, xbuf.at[0], sem.at[0])

    @pl.when(i + 1 < n_valid)
    def _():
        _row_gather(tok_next_ref, tm, h_hbm, xbuf.at[1 - slot], sem.at[1 - slot])

    @pl.when(i < n_valid)
    def _():
        pltpu.make_async_copy(h_hbm.at[pl.ds(0, tm)], xbuf.at[slot], sem.at[slot]).wait()
        x = _rms(xbuf[slot], g_ref[...]).astype(BF16)
        a = jnp.dot(x, wg_ref[0], preferred_element_type=F32)
        b = jnp.dot(x, wu_ref[0], preferred_element_type=F32)
        hmid = (a * _sigmoid(a)) * b
        y_ref[...] = jnp.dot(hmid.astype(BF16), wd_ref[0], preferred_element_type=F32)

    @pl.when(i >= n_valid)
    def _():
        y_ref[...] = jnp.zeros(y_ref.shape, F32)


def _experts(block_e, n_valid, rows_tok, h, g, w_g, w_u, w_d, tm):
    nb = rows_tok.shape[0]
    last = lambda i, nv: jnp.minimum(i, jnp.maximum(nv[0] - 1, 0))
    smem_blk = lambda f: pl.BlockSpec((1, 1, tm), f, memory_space=pltpu.SMEM)
    blocks = 3 * _nbytes((D_MODEL, EXPERT_FF), BF16) + _nbytes((tm, D_MODEL), F32)
    scratch_bytes = 2 * _nbytes((tm, D_MODEL), F32)
    return pl.pallas_call(
        _expert_kernel,
        grid_spec=pltpu.PrefetchScalarGridSpec(
            num_scalar_prefetch=2,
            grid=(nb,),
            in_specs=[
                smem_blk(lambda i, be, nv: (last(i, nv), 0, 0)),
                smem_blk(lambda i, be, nv: (last(i + 1, nv), 0, 0)),
                pl.BlockSpec(memory_space=pl.ANY),
                pl.BlockSpec((1, D_MODEL), lambda i, be, nv: (0, 0)),
                pl.BlockSpec((1, D_MODEL, EXPERT_FF), lambda i, be, nv: (be[last(i, nv)], 0, 0)),
                pl.BlockSpec((1, D_MODEL, EXPERT_FF), lambda i, be, nv: (be[last(i, nv)], 0, 0)),
                pl.BlockSpec((1, EXPERT_FF, D_MODEL), lambda i, be, nv: (be[last(i, nv)], 0, 0)),
            ],
            out_specs=pl.BlockSpec((tm, D_MODEL), lambda i, be, nv: (i, 0)),
            scratch_shapes=[pltpu.VMEM((2, tm, D_MODEL), F32), pltpu.SemaphoreType.DMA((2,))],
        ),
        out_shape=jax.ShapeDtypeStruct((nb * tm, D_MODEL), F32),
        compiler_params=pltpu.CompilerParams(
            dimension_semantics=("arbitrary",), vmem_limit_bytes=_vmem_limit(blocks, scratch_bytes)),
        name="experts",
    )(block_e, n_valid, rows_tok, rows_tok, h, g, w_g, w_u, w_d)


def _combine_kernel(dst_ref, dst_next_ref, yb_hbm, h_ref, gate_ref, y_ref, ybuf, sem):
    tt = h_ref.shape[0]
    n = TOP_K * tt
    i = pl.program_id(0)
    slot = lax.rem(i, 2)

    @pl.when(i == 0)
    def _():
        _row_gather(dst_ref, n, yb_hbm, ybuf.at[0], sem.at[0])

    @pl.when(i + 1 < pl.num_programs(0))
    def _():
        _row_gather(dst_next_ref, n, yb_hbm, ybuf.at[1 - slot], sem.at[1 - slot])

    pltpu.make_async_copy(yb_hbm.at[pl.ds(0, n)], ybuf.at[slot], sem.at[slot]).wait()
    gates = gate_ref[...]
    moe = gates[:, 0:1] * ybuf[slot, 0:tt, :]
    for k in range(1, TOP_K):
        moe = moe + gates[:, k:k + 1] * ybuf[slot, k * tt:(k + 1) * tt, :]
    y_ref[...] = h_ref[...] + moe


def _combine(dest_blocks, yb, h, gates, tt):
    T = h.shape[0]
    n = TOP_K * tt
    nblk = T // tt
    smem_blk = lambda f: pl.BlockSpec((1, 1, n), f, memory_space=pltpu.SMEM)
    blocks = 2 * _nbytes((tt, D_MODEL), F32) + _nbytes((tt, LANES), F32)
    scratch_bytes = 2 * _nbytes((n, D_MODEL), F32)
    return pl.pallas_call(
        _combine_kernel,
        grid=(nblk,),
        in_specs=[smem_blk(lambda i: (i, 0, 0)), smem_blk(lambda i: (jnp.minimum(i + 1, nblk - 1), 0, 0)),
                  pl.BlockSpec(memory_space=pl.ANY),
                  pl.BlockSpec((tt, D_MODEL), lambda i: (i, 0)), pl.BlockSpec((tt, LANES), lambda i: (i, 0))],
        out_specs=pl.BlockSpec((tt, D_MODEL), lambda i: (i, 0)),
        out_shape=jax.ShapeDtypeStruct((T, D_MODEL), F32),
        scratch_shapes=[pltpu.VMEM((2, n, D_MODEL), F32), pltpu.SemaphoreType.DMA((2,))],
        compiler_params=pltpu.CompilerParams(
            dimension_semantics=("arbitrary",), vmem_limit_bytes=_vmem_limit(blocks, scratch_bytes)),
        name="combine",
    )(dest_blocks, dest_blocks, yb, h, gates)


def _moe(h, gates, experts, g, w_g, w_u, w_d, tm, tt):
    T = h.shape[0]
    A = T * TOP_K
    flat_e = experts[:, :TOP_K].reshape(A)
    flat_tok = jnp.repeat(jnp.arange(T, dtype=jnp.int32), TOP_K)
    order = jnp.argsort(flat_e)
    se, stok = flat_e[order], flat_tok[order]
    counts = jnp.bincount(flat_e, length=N_EXPERTS)
    starts = jnp.cumsum(counts) - counts
    pcounts = (counts + tm - 1) // tm * tm
    pends = jnp.cumsum(pcounts)
    pstarts = pends - pcounts
    dest_sorted = (pstarts[se] + jnp.arange(A) - starts[se]).astype(jnp.int32)
    nb = -(-A // tm) + N_EXPERTS
    block_e = jnp.minimum(jnp.sum(pends[None, :] <= (jnp.arange(nb) * tm)[:, None], -1), N_EXPERTS - 1).astype(jnp.int32)
    first = jnp.clip(starts[block_e] + jnp.arange(nb) * tm - pstarts[block_e], 0, A).astype(jnp.int32)
    stok_pad = jnp.concatenate([stok, jnp.zeros((tm,), jnp.int32)])
    rows_tok = jax.vmap(lambda s: lax.dynamic_slice(stok_pad, (s,), (tm,)))(first)
    n_valid = (pends[-1:] // tm).astype(jnp.int32)
    yb = _experts(block_e, n_valid, rows_tok.reshape(nb, 1, tm), h, g, w_g, w_u, w_d, tm)
    dest = dest_sorted[jnp.argsort(order)].reshape(T // tt, tt, TOP_K)
    dest_blocks = dest.transpose(0, 2, 1).reshape(T // tt, 1, TOP_K * tt)
    return _combine(dest_blocks, yb, h, gates, tt)


def _rope_tables(pos):
    half = HEAD_DIM // 2
    inv = ROPE_THETA ** (-jnp.arange(half, dtype=F32) / half)
    ang = pos.astype(F32)[:, None] * inv[None, :]
    cos, sin = jnp.cos(ang), jnp.sin(ang)
    return jnp.concatenate([cos, cos], -1), jnp.concatenate([-sin, sin], -1)


def _layer(x, pos, w, tm, lt, chunk, moe_tm, moe_tt, conv_prev=None, s0=None, win_k=None, win_v=None):
    B, L, D = x.shape
    T = B * L
    x2d = x.reshape(T, D)
    z, zs = _in_proj(x2d, w['norm1_g'], w['w_main'], w['w_small'], min(2 * tm, T))
    z3 = z.reshape(B, L, MAIN_COLS)
    zs3 = zs.reshape(B, L, LANES)
    if conv_prev is None:
        o_a, s_new, conv_new = _delta_prompt(z3, zs3, w['conv_w'], w['a_vec'], w['dt_vec'], w['o_norm_g'], lt)
    else:
        o_a, s_new, conv_new = _delta_mixer(z3, zs3, w['conv_w'], w['a_vec'], w['dt_vec'], w['o_norm_g'], lt, chunk,
                                            conv_prev, s0)
    cos2, sin2 = _rope_tables(pos)
    if win_k is None:
        o_b, wk_new, wv_new = _attn_prompt(z3, cos2, sin2, w['q_norm_g'], w['k_norm_g'])
    else:
        rows = win_k.shape[1] * H_ATTN
        o_b, wk_new, wv_new = _attn_sample(z3, win_k.reshape(B, rows, HEAD_DIM), win_v.reshape(B, rows, HEAD_DIM),
                                           cos2, sin2, w['q_norm_g'], w['k_norm_g'])
    h, gates, experts = _out_proj(o_a.reshape(T, W_DELTA), o_b.reshape(T, W_ATTN), x2d, w['w_out_a'], w['w_out_b'],
                                  w['norm2_g'], w['w_rt_hi'], w['w_rt_lo'], w['b_rt'], tm)
    y = _moe(h, gates, experts, w['norm2_g'], w['w_g'], w['w_u'], w['w_d'], moe_tm, moe_tt)
    keep = wk_new.size // (B * W_ATTN)
    return (y.reshape(B, L, D), conv_new, s_new,
            wk_new.reshape(B, keep, H_ATTN, HEAD_DIM), wv_new.reshape(B, keep, H_ATTN, HEAD_DIM))


def _prep_weights(norm1_g, w_in, conv_w, A_log, dt_bias, o_norm_g, q_norm_g, k_norm_g, w_out, norm2_g,
                  w_group, b_group, w_router, b_router, w_exp_gate, w_exp_up, w_exp_down):
    sizes = [W_DELTA, W_DELTA, W_DELTA, H_DELTA, H_DELTA, W_DELTA, W_ATTN, W_ATTN, W_ATTN]
    zq, zk, zv, zb, za, zg, aq, ak, av = jnp.split(w_in, np.cumsum(sizes)[:-1].tolist(), axis=-1)
    pad = LANES - 2 * H_DELTA
    lane_vec = lambda v: jnp.pad(v.astype(F32), (H_DELTA, LANES - 2 * H_DELTA))[None, :]
    w_rt = jnp.pad(jnp.concatenate([w_group, w_router], -1), ((0, 0), (0, LANES - N_GROUPS - N_EXPERTS)))
    w_rt_hi = w_rt.astype(BF16)
    return dict(
        norm1_g=norm1_g[None, :],
        w_main=jnp.concatenate([zq, zk, zv, zg, aq, ak, av], -1).astype(BF16),
        w_small=jnp.pad(jnp.concatenate([zb, za], -1), ((0, 0), (0, pad))).astype(BF16),
        conv_w=conv_w,
        a_vec=lane_vec(A_log), dt_vec=lane_vec(dt_bias),
        o_norm_g=o_norm_g[None, :], q_norm_g=q_norm_g[None, :], k_norm_g=k_norm_g[None, :],
        w_out_a=w_out[:W_DELTA].astype(BF16), w_out_b=w_out[W_DELTA:].astype(BF16),
        norm2_g=norm2_g[None, :],
        w_rt_hi=w_rt_hi, w_rt_lo=(w_rt - w_rt_hi.astype(F32)).astype(BF16),
        b_rt=jnp.pad(jnp.concatenate([b_group, b_router], -1), (0, LANES - N_GROUPS - N_EXPERTS))[None, :],
        w_g=w_exp_gate.astype(BF16), w_u=w_exp_up.astype(BF16), w_d=w_exp_down.astype(BF16),
    )


def kernel(x_prompt, x_sample, state_conv, state_delta, cache_win_k, cache_win_v, norm1_g, w_in, conv_w, A_log, dt_bias, o_norm_g, q_norm_g, k_norm_g, w_out, norm2_g, w_group, b_group, w_router, b_router, w_exp_gate, w_exp_up, w_exp_down):
    depth = w_in.shape[0]
    pos_p = jnp.arange(x_prompt.shape[1], dtype=jnp.int32)
    pos_s = PAST_LEN + jnp.arange(x_sample.shape[1], dtype=jnp.int32)
    yp, ys = x_prompt, x_sample
    outs = [[] for _ in range(8)]
    for l in range(depth):
        w = _prep_weights(norm1_g[l], w_in[l], conv_w[l], A_log[l], dt_bias[l], o_norm_g[l], q_norm_g[l],
                          k_norm_g[l], w_out[l], norm2_g[l], w_group[l], b_group[l], w_router[l], b_router[l],
                          w_exp_gate[l], w_exp_up[l], w_exp_down[l])
        yp, c1, s1, k1, v1 = _layer(yp, pos_p, w, tm=512, lt=256, chunk=CHUNK, moe_tm=512, moe_tt=256)
        ls = x_sample.shape[1]
        ys, c2, s2, k2, v2 = _layer(ys, pos_s, w, tm=x_sample.shape[0] * ls, lt=ls, chunk=min(CHUNK, ls), moe_tm=128,
                                    moe_tt=128,
                                    conv_prev=state_conv[l], s0=state_delta[l],
                                    win_k=cache_win_k[l], win_v=cache_win_v[l])
        for lst, v in zip(outs, (c1, c2, s1, s2, k1, k2, v1, v2)):
            lst.append(v)
    return (yp, ys) + tuple(jnp.stack(o) for o in outs)
```
